```python
import jax, jax.numpy as jnp
from jax import lax
import numpy as np

D_MODEL = 1024
BATCH = 1
SEQ = 16384
DEPTH = 4

N_MIXERS = 2
N_SSM_LAYERS = (DEPTH + 1) // 2
N_ATTN_LAYERS = DEPTH // 2
SSM_WIDTH = D_MODEL
SSM_GROUP = 16
SSM_GROUPS = SSM_WIDTH // SSM_GROUP
SSM_STATE = 64
DT_MIN = 1e-3
DT_MAX = 1e-1
HEAD_DIM = 128
N_HEADS = D_MODEL // HEAD_DIM
ATTN_WIDTH = N_HEADS * HEAD_DIM
ROT_DIM = HEAD_DIM // 4
ROPE_THETA = 500000.0
MOBA_BLOCK = 256
MOBA_TOPK = 3
Q_CHUNK = 128
NORM_EPS = 1e-6

kernel_name = "hybrid_s5_moba_interleaved"


def rms_norm(x, g):
    xf = x.astype(jnp.float32)
    y = xf * lax.rsqrt(jnp.mean(xf * xf, axis=-1, keepdims=True) + NORM_EPS)
    return (y * g.astype(jnp.float32)).astype(x.dtype)


def partial_rotary(x, pos):
    half = ROT_DIM // 2
    inv_freq = ROPE_THETA ** (-(jnp.arange(half, dtype=jnp.float32) * 2.0) / ROT_DIM)
    ang = pos.astype(jnp.float32)[:, None] * inv_freq[None, :]
    cos, sin = jnp.cos(ang), jnp.sin(ang)
    xf = x.astype(jnp.float32)
    x1, x2, rest = xf[..., :half], xf[..., half:ROT_DIM], xf[..., ROT_DIM:]
    out = jnp.concatenate([x1 * cos - x2 * sin, x2 * cos + x1 * sin, rest], axis=-1)
    return out.astype(x.dtype)


def _complex_linear_combine(e1, e2):
    a1r, a1i, b1r, b1i = e1
    a2r, a2i, b2r, b2i = e2
    ar = a2r * a1r - a2i * a1i
    ai = a2r * a1i + a2i * a1r
    br = a2r * b1r - a2i * b1i + b2r
    bi = a2r * b1i + a2i * b1r + b2i
    return (ar, ai, br, bi)


def s5_mixer(h, w_in, a_re, a_im, log_dt, b_re, b_im, c_re, c_im, d_skip, w_glu, b_glu, w_out):
    bsz, seq, _ = h.shape
    proj = h @ w_in
    u, z = proj[..., :SSM_WIDTH], proj[..., SSM_WIDTH:]
    uf = u.astype(jnp.float32)
    ug = uf.reshape(bsz, seq, SSM_GROUPS, SSM_GROUP)
    dt = jnp.exp(log_dt.astype(jnp.float32))[:, None]
    lr, li = a_re.astype(jnp.float32), a_im.astype(jnp.float32)
    mag = jnp.exp(lr * dt)
    ab_re, ab_im = mag * jnp.cos(li * dt), mag * jnp.sin(li * dt)
    den = lr * lr + li * li
    nr, ni = ab_re - 1.0, ab_im
    f_re = (nr * lr + ni * li) / den
    f_im = (ni * lr - nr * li) / den
    bu_re = jnp.einsum('blgc,gpc->blgp', ug, b_re.astype(jnp.float32))
    bu_im = jnp.einsum('blgc,gpc->blgp', ug, b_im.astype(jnp.float32))
    in_re = f_re * bu_re - f_im * bu_im
    in_im = f_re * bu_im + f_im * bu_re
    a_re_t = jnp.broadcast_to(ab_re, in_re.shape)
    a_im_t = jnp.broadcast_to(ab_im, in_re.shape)
    _, _, s_re, s_im = lax.associative_scan(
        _complex_linear_combine, (a_re_t, a_im_t, in_re, in_im), axis=1)
    y = (jnp.einsum('blgp,gcp->blgc', s_re, c_re.astype(jnp.float32))
         - jnp.einsum('blgp,gcp->blgc', s_im, c_im.astype(jnp.float32)))
    y = y.reshape(bsz, seq, SSM_WIDTH) + d_skip.astype(jnp.float32) * uf
    y = jax.nn.gelu(y)
    y = y * jax.nn.sigmoid(y @ w_glu.astype(jnp.float32) + b_glu.astype(jnp.float32))
    y = y.astype(h.dtype) * jax.nn.silu(z)
    return y @ w_out


def moba_mixer(h, w_in, q_gain, k_gain, w_out):
    bsz, seq, _ = h.shape
    proj = h @ w_in
    q, k, v, z = jnp.split(proj, 4, axis=-1)

    def heads(t):
        return t.reshape(bsz, seq, N_HEADS, HEAD_DIM).transpose(0, 2, 1, 3)

    pos = jnp.arange(seq)
    q = partial_rotary(rms_norm(heads(q), q_gain), pos)
    k = partial_rotary(rms_norm(heads(k), k_gain), pos)
    v = heads(v)
    n_blocks = -(-seq // MOBA_BLOCK)
    pad = n_blocks * MOBA_BLOCK - seq
    k_pad = jnp.pad(k, ((0, 0), (0, 0), (0, pad), (0, 0)))
    v_pad = jnp.pad(v, ((0, 0), (0, 0), (0, pad), (0, 0)))
    k_blocks = k_pad.reshape(bsz, N_HEADS, n_blocks, MOBA_BLOCK, HEAD_DIM)
    v_blocks = v_pad.reshape(bsz, N_HEADS, n_blocks, MOBA_BLOCK, HEAD_DIM)
    k_mean = jnp.mean(k_blocks.astype(jnp.float32), axis=3)
    top_k = min(MOBA_TOPK, n_blocks)
    scale = HEAD_DIM ** -0.5
    b_idx = jnp.arange(bsz)[:, None, None, None]
    h_idx = jnp.arange(N_HEADS)[None, :, None, None]
    blk_ids = jnp.arange(n_blocks)
    key_off = jnp.arange(MOBA_BLOCK)

    def chunk(c):
        start = c * Q_CHUNK
        qc = lax.dynamic_slice_in_dim(q, start, Q_CHUNK, axis=2).astype(jnp.float32)
        own = start // MOBA_BLOCK
        q_pos = start + jnp.arange(Q_CHUNK)
        gate = jnp.einsum('bhqd,bhnd->bhqn', qc, k_mean)
        gate = jnp.where(blk_ids[None, None, None, :] < own, gate, -jnp.inf)
        _, sel = lax.top_k(gate, top_k)
        sel_valid = sel < own
        k_sel = k_blocks[b_idx, h_idx, sel].astype(jnp.float32)
        v_sel = v_blocks[b_idx, h_idx, sel].astype(jnp.float32)
        s_sel = jnp.einsum('bhqd,bhqnkd->bhqnk', qc, k_sel) * scale
        s_sel = jnp.where(sel_valid[..., None], s_sel, -jnp.inf)
        k_own = lax.dynamic_slice_in_dim(k_pad, own * MOBA_BLOCK, MOBA_BLOCK, axis=2).astype(jnp.float32)
        v_own = lax.dynamic_slice_in_dim(v_pad, own * MOBA_BLOCK, MOBA_BLOCK, axis=2).astype(jnp.float32)
        s_own = jnp.einsum('bhqd,bhkd->bhqk', qc, k_own) * scale
        causal = (own * MOBA_BLOCK + key_off)[None, :] <= q_pos[:, None]
        s_own = jnp.where(causal, s_own, -jnp.inf)
        logits = jnp.concatenate(
            [s_own, s_sel.reshape(bsz, N_HEADS, Q_CHUNK, top_k * MOBA_BLOCK)], axis=-1)
        p = jax.nn.softmax(logits, axis=-1)
        p_own = p[..., :MOBA_BLOCK]
        p_sel = p[..., MOBA_BLOCK:].reshape(bsz, N_HEADS, Q_CHUNK, top_k, MOBA_BLOCK)
        o = (jnp.einsum('bhqk,bhkd->bhqd', p_own, v_own)
             + jnp.einsum('bhqnk,bhqnkd->bhqd', p_sel, v_sel))
        return o.astype(h.dtype)

    out = lax.map(chunk, jnp.arange(seq // Q_CHUNK))
    out = out.transpose(1, 0, 3, 2, 4).reshape(bsz, seq, ATTN_WIDTH)
    return (out * jax.nn.silu(z)) @ w_out


def setup_inputs(seed: int = 0) -> dict:
    key = jax.random.key(seed)
    ks = jax.random.split(key, 20)
    f32 = jnp.float32
    na, nb = N_SSM_LAYERS, N_ATTN_LAYERS
    G, P, C = SSM_GROUPS, SSM_STATE, SSM_GROUP
    x = jax.random.normal(ks[0], (BATCH, SEQ, D_MODEL), f32)
    norm_g = 1.0 + 0.02 * jax.random.normal(ks[1], (DEPTH, D_MODEL), f32)
    ssm_w_in = jax.random.normal(ks[2], (na, D_MODEL, 2 * SSM_WIDTH), f32) * D_MODEL ** -0.5
    ssm_a_re = -0.5 + 0.01 * jax.random.normal(ks[3], (na, G, P), f32)
    ssm_a_im = (jnp.pi * jnp.arange(P, dtype=f32))[None, None, :] + 0.01 * jax.random.normal(ks[4], (na, G, P), f32)
    ssm_log_dt = jax.random.uniform(ks[5], (na, G), f32, minval=float(np.log(DT_MIN)), maxval=float(np.log(DT_MAX)))
    ssm_b_re = jax.random.normal(ks[6], (na, G, P, C), f32) * (2 * C) ** -0.5
    ssm_b_im = jax.random.normal(ks[7], (na, G, P, C), f32) * (2 * C) ** -0.5
    ssm_c_re = jax.random.normal(ks[8], (na, G, C, P), f32) * P ** -0.5
    ssm_c_im = jax.random.normal(ks[9], (na, G, C, P), f32) * P ** -0.5
    ssm_d = jax.random.normal(ks[10], (na, SSM_WIDTH), f32)
    ssm_w_glu = jax.random.normal(ks[11], (na, SSM_WIDTH, SSM_WIDTH), f32) * SSM_WIDTH ** -0.5
    ssm_b_glu = 0.01 * jax.random.normal(ks[12], (na, SSM_WIDTH), f32)
    ssm_w_out = jax.random.normal(ks[13], (na, SSM_WIDTH, D_MODEL), f32) * SSM_WIDTH ** -0.5
    attn_w_in = jax.random.normal(ks[14], (nb, D_MODEL, 4 * ATTN_WIDTH), f32) * D_MODEL ** -0.5
    attn_q_gain = 1.0 + 0.02 * jax.random.normal(ks[15], (nb, HEAD_DIM), f32)
    attn_k_gain = 1.0 + 0.02 * jax.random.normal(ks[16], (nb, HEAD_DIM), f32)
    attn_w_out = jax.random.normal(ks[17], (nb, ATTN_WIDTH, D_MODEL), f32) * ATTN_WIDTH ** -0.5
    return {"x": x, "norm_g": norm_g, "ssm_w_in": ssm_w_in, "ssm_a_re": ssm_a_re,
            "ssm_a_im": ssm_a_im, "ssm_log_dt": ssm_log_dt, "ssm_b_re": ssm_b_re,
            "ssm_b_im": ssm_b_im, "ssm_c_re": ssm_c_re, "ssm_c_im": ssm_c_im,
            "ssm_d": ssm_d, "ssm_w_glu": ssm_w_glu, "ssm_b_glu": ssm_b_glu,
            "ssm_w_out": ssm_w_out, "attn_w_in": attn_w_in, "attn_q_gain": attn_q_gain,
            "attn_k_gain": attn_k_gain, "attn_w_out": attn_w_out}


def reference(x, norm_g, ssm_w_in, ssm_a_re, ssm_a_im, ssm_log_dt, ssm_b_re, ssm_b_im,
              ssm_c_re, ssm_c_im, ssm_d, ssm_w_glu, ssm_b_glu, ssm_w_out,
              attn_w_in, attn_q_gain, attn_k_gain, attn_w_out):
    h = x
    for i in range(DEPTH):
        hn = rms_norm(h, norm_g[i])
        j = i // N_MIXERS
        if i % N_MIXERS == 0:
            y = s5_mixer(hn, ssm_w_in[j], ssm_a_re[j], ssm_a_im[j], ssm_log_dt[j],
                         ssm_b_re[j], ssm_b_im[j], ssm_c_re[j], ssm_c_im[j], ssm_d[j],
                         ssm_w_glu[j], ssm_b_glu[j], ssm_w_out[j])
        else:
            y = moba_mixer(hn, attn_w_in[j], attn_q_gain[j], attn_k_gain[j], attn_w_out[j])
        h = h + y
    return h
```

```python
import functools
import math

import jax
import jax.numpy as jnp
from jax import lax
from jax.experimental import pallas as pl
from jax.experimental.pallas import tpu as pltpu

F32 = jnp.float32
BF16 = jnp.bfloat16

NORM_EPS = 1e-6
SSM_GROUP = 16
SSM_STATE = 64
HEAD_DIM = 128
ROT_DIM = HEAD_DIM // 4
ROPE_THETA = 500000.0
MOBA_BLOCK = 256
MOBA_TOPK = 3

LANES = 128
SLAB_GROUPS = LANES // SSM_GROUP
SSM_CHUNK = 16
NEG_BIG = -(2.0 ** 60)
VMEM_LIMIT = 56 * 1024 * 1024

_NT = (((1,), (1,)), ((), ()))


def _cparams(*sem):
    return pltpu.CompilerParams(dimension_semantics=sem, vmem_limit_bytes=VMEM_LIMIT)


def _rms(x, g):
    ms = jnp.mean(x * x, axis=-1, keepdims=True)
    return x * lax.rsqrt(ms + NORM_EPS) * g


def _s5_in_kernel(h_ref, g_ref, w_ref, u_ref, z_ref):
    d = h_ref.shape[1]
    hn = _rms(h_ref[...], g_ref[...]).astype(BF16)
    proj = jnp.dot(hn, w_ref[...], preferred_element_type=F32)
    for s in range(d // LANES):
        u_ref[s] = proj[:, s * LANES:(s + 1) * LANES]
    z_ref[...] = proj[:, d:]


def _ssm_state_kernel(u_ref, wz_ref, z_ref):
    z_ref[...] = jnp.dot(u_ref[...].astype(BF16), wz_ref[...], preferred_element_type=F32)


def _ssm_scan_kernel(z_ref, a_ref, o_ref, st_ref):
    half = a_ref.shape[1] // 2

    @pl.when(pl.program_id(0) == 0)
    def _():
        st_ref[...] = jnp.zeros_like(st_ref)

    ar = a_ref[:, :half]
    ai = a_ref[:, half:]

    def body(c, carry):
        sr, si = carry
        o_ref[c, :, :half] = sr
        o_ref[c, :, half:] = si
        zc = z_ref[c]
        nr = ar * sr - ai * si + zc[:, :half]
        ni = ar * si + ai * sr + zc[:, half:]
        return nr, ni

    sr, si = lax.fori_loop(0, z_ref.shape[0], body, (st_ref[:, :half], st_ref[:, half:]))
    st_ref[:, :half] = sr
    st_ref[:, half:] = si


def _ssm_out_kernel(u_ref, m_ref, sp_ref, v_ref, d_ref, y_ref):
    u = u_ref[...]
    y = jnp.dot(u.astype(BF16), m_ref[...], preferred_element_type=F32)
    y = y + jnp.dot(sp_ref[...].astype(BF16), v_ref[...], preferred_element_type=F32)
    y = y + d_ref[...] * u
    y_ref[...] = jax.nn.gelu(y, approximate=True)


def _s5_post_kernel(y_ref, z_ref, h_ref, wg_ref, bg_ref, wo_ref, o_ref):
    y = jnp.concatenate([y_ref[s] for s in range(y_ref.shape[0])], axis=-1)
    lin = jnp.dot(y.astype(BF16), wg_ref[...], preferred_element_type=F32) + bg_ref[...]
    gated = y * jax.nn.sigmoid(lin) * jax.nn.silu(z_ref[...])
    o_ref[...] = h_ref[...] + jnp.dot(gated.astype(BF16), wo_ref[...], preferred_element_type=F32)


def _s5_derived(a_re, a_im, log_dt, b_re, b_im, c_re, c_im, d_skip):
    hp = lax.Precision.HIGHEST
    t_len = SSM_CHUNK
    g_cnt, p_cnt = a_re.shape
    n_slab = g_cnt // SLAB_GROUPS
    dt = jnp.exp(log_dt)[:, None]
    mag = jnp.exp(a_re * dt)
    ab_re, ab_im = mag * jnp.cos(a_im * dt), mag * jnp.sin(a_im * dt)
    den = a_re * a_re + a_im * a_im
    nr, ni = ab_re - 1.0, ab_im
    f_re = (nr * a_re + ni * a_im) / den
    f_im = (ni * a_re - nr * a_im) / den
    pw_re, pw_im = [jnp.ones_like(ab_re)], [jnp.zeros_like(ab_re)]
    for _ in range(t_len):
        pr, pi = pw_re[-1], pw_im[-1]
        pw_re.append(pr * ab_re - pi * ab_im)
        pw_im.append(pr * ab_im + pi * ab_re)
    pw_re, pw_im = jnp.stack(pw_re), jnp.stack(pw_im)
    w_re = pw_re[:t_len] * f_re - pw_im[:t_len] * f_im
    w_im = pw_re[:t_len] * f_im + pw_im[:t_len] * f_re
    fb_re = w_re[..., None] * b_re - w_im[..., None] * b_im
    fb_im = w_re[..., None] * b_im + w_im[..., None] * b_re
    kern = (jnp.einsum('gcp,jgpd->jgcd', c_re, fb_re, precision=hp)
            - jnp.einsum('gcp,jgpd->jgcd', c_im, fb_im, precision=hp))
    eye = jnp.eye(SLAB_GROUPS, dtype=F32)
    lag = jnp.arange(t_len)[None, :] - jnp.arange(t_len)[:, None]
    ktoe = jnp.where((lag >= 0)[:, :, None, None, None], kern[jnp.clip(lag, 0)], 0.0)
    ktoe = ktoe.reshape(t_len, t_len, n_slab, SLAB_GROUPS, SSM_GROUP, SSM_GROUP)
    m_mat = jnp.einsum('ktsgcd,gh->skgdthc', ktoe, eye).reshape(
        n_slab, t_len * LANES, t_len * LANES)
    fbz = jnp.stack([fb_re[::-1], fb_im[::-1]])
    fbz = fbz.reshape(2, t_len, n_slab, SLAB_GROUPS, p_cnt, SSM_GROUP)
    wz = jnp.einsum('rksgpd,gh->skgdrhp', fbz, eye).reshape(
        n_slab, t_len * LANES, 2 * SLAB_GROUPS * p_cnt)
    cp_re = c_re[None] * pw_re[1:, :, None, :] - c_im[None] * pw_im[1:, :, None, :]
    cp_im = c_re[None] * pw_im[1:, :, None, :] + c_im[None] * pw_re[1:, :, None, :]
    vv = jnp.stack([cp_re, -cp_im]).reshape(2, t_len, n_slab, SLAB_GROUPS, SSM_GROUP, p_cnt)
    v_mat = jnp.einsum('rtsgcp,gh->srgpthc', vv, eye).reshape(
        n_slab, 2 * SLAB_GROUPS * p_cnt, t_len * LANES)
    a_t = jnp.concatenate([pw_re[t_len].reshape(n_slab, -1), pw_im[t_len].reshape(n_slab, -1)], axis=1)
    d_til = jnp.tile(d_skip.reshape(n_slab, 1, LANES), (1, 1, t_len))
    return m_mat.astype(BF16), wz.astype(BF16), v_mat.astype(BF16), a_t, d_til


def _s5_layer(h, g, w_in, a_re, a_im, log_dt, b_re, b_im, c_re, c_im, d_skip, w_glu, b_glu, w_out):
    seq, d = h.shape
    n_slab = d // LANES
    t_len = SSM_CHUNK
    n_chunk = seq // t_len
    tw = t_len * LANES
    sw = 2 * SLAB_GROUPS * SSM_STATE
    m_mat, wz, v_mat, a_t, d_til = _s5_derived(a_re, a_im, log_dt, b_re, b_im, c_re, c_im, d_skip)

    tm = 512
    u, z = pl.pallas_call(
        _s5_in_kernel,
        grid=(seq // tm,),
        in_specs=[pl.BlockSpec((tm, d), lambda i: (i, 0)),
                  pl.BlockSpec((1, d), lambda i: (0, 0)),
                  pl.BlockSpec((d, 2 * d), lambda i: (0, 0))],
        out_specs=[pl.BlockSpec((n_slab, tm, LANES), lambda i: (0, i, 0)),
                   pl.BlockSpec((tm, d), lambda i: (i, 0))],
        out_shape=[jax.ShapeDtypeStruct((n_slab, seq, LANES), F32),
                   jax.ShapeDtypeStruct((seq, d), F32)],
        compiler_params=_cparams("parallel"),
        name="s5_in",
    )(h, g.reshape(1, d), w_in.astype(BF16))
    uc = u.reshape(n_slab, n_chunk, tw)

    tr = 512
    zst = pl.pallas_call(
        _ssm_state_kernel,
        grid=(n_slab, n_chunk // tr),
        in_specs=[pl.BlockSpec((None, tr, tw), lambda s, i: (s, i, 0)),
                  pl.BlockSpec((None, tw, sw), lambda s, i: (s, 0, 0))],
        out_specs=pl.BlockSpec((tr, sw), lambda s, i: (i, s)),
        out_shape=jax.ShapeDtypeStruct((n_chunk, n_slab * sw), F32),
        compiler_params=_cparams("parallel", "parallel"),
        name="ssm_state",
    )(uc, wz)

    tc = 256
    sprev = pl.pallas_call(
        _ssm_scan_kernel,
        grid=(n_chunk // tc,),
        in_specs=[pl.BlockSpec((tc, n_slab, sw), lambda i: (i, 0, 0)),
                  pl.BlockSpec((n_slab, sw), lambda i: (0, 0))],
        out_specs=pl.BlockSpec((tc, n_slab, sw), lambda i: (i, 0, 0)),
        out_shape=jax.ShapeDtypeStruct((n_chunk, n_slab, sw), F32),
        scratch_shapes=[pltpu.VMEM((n_slab, sw), F32)],
        compiler_params=_cparams("arbitrary"),
        name="ssm_scan",
    )(zst.reshape(n_chunk, n_slab, sw), a_t)

    yg = pl.pallas_call(
        _ssm_out_kernel,
        grid=(n_slab, n_chunk // tr),
        in_specs=[pl.BlockSpec((None, tr, tw), lambda s, i: (s, i, 0)),
                  pl.BlockSpec((None, tw, tw), lambda s, i: (s, 0, 0)),
                  pl.BlockSpec((tr, sw), lambda s, i: (i, s)),
                  pl.BlockSpec((None, sw, tw), lambda s, i: (s, 0, 0)),
                  pl.BlockSpec((None, 1, tw), lambda s, i: (s, 0, 0))],
        out_specs=pl.BlockSpec((None, tr, tw), lambda s, i: (s, i, 0)),
        out_shape=jax.ShapeDtypeStruct((n_slab, n_chunk, tw), F32),
        compiler_params=_cparams("parallel", "parallel"),
        name="ssm_out",
    )(uc, m_mat, sprev.reshape(n_chunk, n_slab * sw), v_mat, d_til)

    return pl.pallas_call(
        _s5_post_kernel,
        grid=(seq // tm,),
        in_specs=[pl.BlockSpec((n_slab, tm, LANES), lambda i: (0, i, 0)),
                  pl.BlockSpec((tm, d), lambda i: (i, 0)),
                  pl.BlockSpec((tm, d), lambda i: (i, 0)),
                  pl.BlockSpec((d, d), lambda i: (0, 0)),
                  pl.BlockSpec((1, d), lambda i: (0, 0)),
                  pl.BlockSpec((d, d), lambda i: (0, 0))],
        out_specs=pl.BlockSpec((tm, d), lambda i: (i, 0)),
        out_shape=jax.ShapeDtypeStruct((seq, d), F32),
        compiler_params=_cparams("parallel"),
        name="s5_post",
    )(yg.reshape(n_slab, seq, LANES), z, h, w_glu.astype(BF16), b_glu.reshape(1, d), w_out.astype(BF16))


def _attn_in_kernel(h_ref, g_ref, wqkz_ref, wvt_ref, qg_ref, kg_ref, cos_ref, sa_ref, sb_ref,
                    q_ref, ka_ref, km_ref, vt_ref, z_ref):
    tm, d = h_ref.shape
    n_head = d // HEAD_DIM
    n_blk = tm // MOBA_BLOCK
    i = pl.program_id(0)
    hn = _rms(h_ref[...], g_ref[...]).astype(BF16)
    qkz = jnp.dot(hn, wqkz_ref[...], preferred_element_type=F32)
    vt = lax.dot_general(wvt_ref[...], hn, _NT, preferred_element_type=F32)
    cosf, sa, sb = cos_ref[...], sa_ref[...], sb_ref[...]

    def norm_rope(t, gain):
        t = _rms(t, gain)
        up = pltpu.roll(t, HEAD_DIM - ROT_DIM // 2, 1)
        dn = pltpu.roll(t, ROT_DIM // 2, 1)
        return t * cosf + up * sa + dn * sb

    scale = HEAD_DIM ** -0.5
    for hd in range(n_head):
        qh = norm_rope(qkz[:, hd * HEAD_DIM:(hd + 1) * HEAD_DIM], qg_ref[...])
        q_ref[hd] = qh * scale
        kh = norm_rope(qkz[:, d + hd * HEAD_DIM:d + (hd + 1) * HEAD_DIM], kg_ref[...])
        ka_ref[hd, :, :HEAD_DIM] = kh.astype(BF16)
        for b in range(n_blk):
            km_ref[b, hd:hd + 1, :] = jnp.mean(
                kh[b * MOBA_BLOCK:(b + 1) * MOBA_BLOCK], axis=0, keepdims=True)
    lane = lax.broadcasted_iota(jnp.int32, (MOBA_BLOCK, LANES), 1)
    for b in range(n_blk):
        onehot = jnp.where(lane == i * n_blk + b, NEG_BIG, 0.0).astype(BF16)
        for hd in range(n_head):
            ka_ref[hd, b * MOBA_BLOCK:(b + 1) * MOBA_BLOCK, HEAD_DIM:] = onehot
        vt_ref[b] = vt[:, b * MOBA_BLOCK:(b + 1) * MOBA_BLOCK].astype(BF16)
    z_ref[...] = qkz[:, 2 * d:]


def _moba_kernel(q_ref, k_ref, vt_ref, km_ref, o_ref, qa_ref):
    own = pl.program_id(1)
    tq = q_ref.shape[0]
    q = q_ref[...]
    gates = lax.dot_general(km_ref[...], q, _NT, precision=lax.Precision.HIGHEST,
                            preferred_element_type=F32)
    nidx = lax.broadcasted_iota(jnp.int32, gates.shape, 0)
    valid = nidx < own
    g = jnp.where(valid, gates, -jnp.inf)
    picked = jnp.zeros(gates.shape, jnp.bool_)
    for _ in range(MOBA_TOPK):
        mx = jnp.max(g, axis=0, keepdims=True)
        first = jnp.min(jnp.where(g == mx, nidx, gates.shape[0]), axis=0, keepdims=True)
        hit = nidx == first
        picked = jnp.logical_or(picked, hit)
        g = jnp.where(hit, -jnp.inf, g)
    notsel = jnp.where(jnp.logical_and(valid, jnp.logical_not(picked)), 1.0, 0.0)
    qa_ref[:, :HEAD_DIM] = q.astype(BF16)
    qa_ref[:, HEAD_DIM:] = notsel.T.astype(BF16)
    qa = qa_ref[...]

    def scores(n):
        kt = k_ref[pl.ds(pl.multiple_of(n * MOBA_BLOCK, MOBA_BLOCK), MOBA_BLOCK), :]
        return lax.dot_general(kt, qa, _NT, preferred_element_type=F32)

    s = scores(own)
    kj = lax.broadcasted_iota(jnp.int32, s.shape, 0)
    qi = lax.broadcasted_iota(jnp.int32, s.shape, 1)
    s = jnp.where(kj <= qi, s, -jnp.inf)
    m = jnp.max(s, axis=0, keepdims=True)
    p = jnp.exp(s - m)
    l = jnp.sum(p, axis=0, keepdims=True)
    acc = jnp.dot(vt_ref[own], p.astype(BF16), preferred_element_type=F32)

    def body(n, carry):
        m, l, acc = carry
        s = scores(n)
        m_new = jnp.maximum(m, jnp.max(s, axis=0, keepdims=True))
        alpha = jnp.exp(m - m_new)
        p = jnp.exp(s - m_new)
        l = alpha * l + jnp.sum(p, axis=0, keepdims=True)
        acc = alpha * acc + jnp.dot(vt_ref[n], p.astype(BF16), preferred_element_type=F32)
        return m_new, l, acc

    m, l, acc = lax.fori_loop(0, own, body, (m, l, acc))
    o_ref[...] = (acc / l).T


def _attn_post_kernel(o_ref, z_ref, h_ref, wo_ref, out_ref):
    gated = o_ref[...] * jax.nn.silu(z_ref[...])
    out_ref[...] = h_ref[...] + jnp.dot(gated.astype(BF16), wo_ref[...], preferred_element_type=F32)


def _rope_tables(seq):
    half = ROT_DIM // 2
    inv_freq = ROPE_THETA ** (-(jnp.arange(half, dtype=F32) * 2.0) / ROT_DIM)
    ang = jnp.arange(seq).astype(F32)[:, None] * inv_freq[None, :]
    cos, sin = jnp.cos(ang), jnp.sin(ang)
    pad = HEAD_DIM - ROT_DIM
    zeros = jnp.zeros((seq, half), F32)
    cosf = jnp.concatenate([cos, cos, jnp.ones((seq, pad), F32)], axis=1)
    sa = jnp.concatenate([-sin, zeros, jnp.zeros((seq, pad), F32)], axis=1)
    sb = jnp.concatenate([zeros, sin, jnp.zeros((seq, pad), F32)], axis=1)
    return cosf, sa, sb


def _moba_layer(h, g, w_in, q_gain, k_gain, w_out, rope):
    seq, d = h.shape
    n_head = d // HEAD_DIM
    n_blk = seq // MOBA_BLOCK
    assert n_blk <= LANES and seq % MOBA_BLOCK == 0
    cosf, sa, sb = rope
    w_qkz = jnp.concatenate([w_in[:, :2 * d], w_in[:, 3 * d:]], axis=1).astype(BF16)
    w_vt = w_in[:, 2 * d:3 * d].T.astype(BF16)

    tm = 512
    bpt = tm // MOBA_BLOCK
    row = lambda i: (i, 0)
    const = lambda i: (0, 0)
    q, kaug, kmean, vt, z = pl.pallas_call(
        _attn_in_kernel,
        grid=(seq // tm,),
        in_specs=[pl.BlockSpec((tm, d), row),
                  pl.BlockSpec((1, d), const),
                  pl.BlockSpec((d, 3 * d), const),
                  pl.BlockSpec((d, d), const),
                  pl.BlockSpec((1, HEAD_DIM), const),
                  pl.BlockSpec((1, HEAD_DIM), const),
                  pl.BlockSpec((tm, HEAD_DIM), row),
                  pl.BlockSpec((tm, HEAD_DIM), row),
                  pl.BlockSpec((tm, HEAD_DIM), row)],
        out_specs=[pl.BlockSpec((n_head, tm, HEAD_DIM), lambda i: (0, i, 0)),
                   pl.BlockSpec((n_head, tm, 2 * HEAD_DIM), lambda i: (0, i, 0)),
                   pl.BlockSpec((bpt, n_head, HEAD_DIM), lambda i: (i, 0, 0)),
                   pl.BlockSpec((bpt, d, MOBA_BLOCK), lambda i: (i, 0, 0)),
                   pl.BlockSpec((tm, d), row)],
        out_shape=[jax.ShapeDtypeStruct((n_head, seq, HEAD_DIM), F32),
                   jax.ShapeDtypeStruct((n_head, seq, 2 * HEAD_DIM), BF16),
                   jax.ShapeDtypeStruct((n_blk, n_head, HEAD_DIM), F32),
                   jax.ShapeDtypeStruct((n_blk, d, MOBA_BLOCK), BF16),
                   jax.ShapeDtypeStruct((seq, d), F32)],
        compiler_params=_cparams("parallel"),
        name="attn_in",
    )(h, g.reshape(1, d), w_qkz, w_vt, q_gain.reshape(1, HEAD_DIM), k_gain.reshape(1, HEAD_DIM),
      cosf, sa, sb)

    km = jnp.pad(kmean.transpose(1, 0, 2), ((0, 0), (0, LANES - n_blk), (0, 0)))

    o = pl.pallas_call(
        _moba_kernel,
        grid=(n_head, n_blk),
        in_specs=[pl.BlockSpec((None, MOBA_BLOCK, HEAD_DIM), lambda hd, i: (hd, i, 0)),
                  pl.BlockSpec((None, seq, 2 * HEAD_DIM), lambda hd, i: (hd, 0, 0)),
                  pl.BlockSpec((n_blk, HEAD_DIM, MOBA_BLOCK), lambda hd, i: (0, hd, 0)),
                  pl.BlockSpec((None, LANES, HEAD_DIM), lambda hd, i: (hd, 0, 0))],
        out_specs=pl.BlockSpec((MOBA_BLOCK, HEAD_DIM), lambda hd, i: (i, hd)),
        out_shape=jax.ShapeDtypeStruct((seq, d), F32),
        scratch_shapes=[pltpu.VMEM((MOBA_BLOCK, 2 * HEAD_DIM), BF16)],
        compiler_params=_cparams("parallel", "arbitrary"),
        name="moba_attn",
    )(q, kaug, vt, km)

    return pl.pallas_call(
        _attn_post_kernel,
        grid=(seq // tm,),
        in_specs=[pl.BlockSpec((tm, d), row),
                  pl.BlockSpec((tm, d), row),
                  pl.BlockSpec((tm, d), row),
                  pl.BlockSpec((d, d), const)],
        out_specs=pl.BlockSpec((tm, d), row),
        out_shape=jax.ShapeDtypeStruct((seq, d), F32),
        compiler_params=_cparams("parallel"),
        name="attn_post",
    )(o, z, h, w_out.astype(BF16))


def kernel(x, norm_g, ssm_w_in, ssm_a_re, ssm_a_im, ssm_log_dt, ssm_b_re, ssm_b_im, ssm_c_re, ssm_c_im, ssm_d, ssm_w_glu, ssm_b_glu, ssm_w_out, attn_w_in, attn_q_gain, attn_k_gain, attn_w_out):
    bsz, seq, d = x.shape
    depth = norm_g.shape[0]
    rope = _rope_tables(seq)
    outs = []
    for b in range(bsz):
        h = x[b]
        for i in range(depth):
            j = i // 2
            if i % 2 == 0:
                h = _s5_layer(h, norm_g[i], ssm_w_in[j], ssm_a_re[j], ssm_a_im[j], ssm_log_dt[j],
                              ssm_b_re[j], ssm_b_im[j], ssm_c_re[j], ssm_c_im[j], ssm_d[j],
                              ssm_w_glu[j], ssm_b_glu[j], ssm_w_out[j])
            else:
                h = _moba_layer(h, norm_g[i], attn_w_in[j], attn_q_gain[j], attn_k_gain[j],
                                attn_w_out[j], rope)
        outs.append(h)
    return jnp.stack(outs)
```

```python
import jax
import jax.numpy as jnp
from jax import lax
from jax.experimental import pallas as pl
from jax.experimental.pallas import tpu as pltpu

F32 = jnp.float32
BF16 = jnp.bfloat16

NORM_EPS = 1e-6
SSM_GROUP = 16
SSM_STATE = 64
HEAD_DIM = 128
ROT_DIM = HEAD_DIM // 4
ROPE_THETA = 500000.0
MOBA_BLOCK = 256
MOBA_TOPK = 3

LANES = 128
SUBLANES = 8
SLAB_GROUPS = LANES // SSM_GROUP
SLAB_STATE = 2 * SLAB_GROUPS * SSM_STATE
SSM_CHUNK = 16
KV_GROUP = 4
NEG_BIG = -(2.0 ** 60)
VMEM_LIMIT = 56 * 1024 * 1024

_NT = (((1,), (1,)), ((), ()))


def _cparams(*sem):
    return pltpu.CompilerParams(dimension_semantics=sem, vmem_limit_bytes=VMEM_LIMIT)


def _rms(x, g):
    ms = jnp.mean(x * x, axis=-1, keepdims=True)
    return x * lax.rsqrt(ms + NORM_EPS) * g


def _log2(n):
    assert n & (n - 1) == 0
    return n.bit_length() - 1


def _s5_in_kernel(h_ref, g_ref, w_ref, u_ref, z_ref, us_ref):
    tm, d = h_ref.shape
    n_row = tm // SSM_CHUNK
    hn = _rms(h_ref[...], g_ref[...]).astype(BF16)
    proj = jnp.dot(hn, w_ref[...], preferred_element_type=F32)
    z_ref[...] = proj[:, d:]
    for s in range(d // LANES):
        us_ref[s] = proj[:, s * LANES:(s + 1) * LANES]
        for t in range(SSM_CHUNK):
            u_ref[s, :, t * LANES:(t + 1) * LANES] = us_ref[s, pl.ds(t, n_row, stride=SSM_CHUNK), :]


def _expand_blockdiag(re2_ref, im2_ref, out_ref):
    shape = re2_ref.shape
    r = lax.broadcasted_iota(jnp.int32, shape, 0)
    lane = lax.broadcasted_iota(jnp.int32, shape, 1)
    row_group = (r >> _log2(SSM_GROUP)) & (SLAB_GROUPS - 1)
    lane_half = lane >> _log2(SSM_STATE)
    re2, im2 = re2_ref[...], im2_ref[...]
    half = SLAB_STATE // 2
    for j in range(half // LANES):
        msk = row_group == 2 * j + lane_half
        out_ref[:, j * LANES:(j + 1) * LANES] = jnp.where(msk, re2, 0.0).astype(BF16)
        out_ref[:, half + j * LANES:half + (j + 1) * LANES] = jnp.where(msk, im2, 0.0).astype(BF16)


def _ssm_state_kernel(u_ref, re2_ref, im2_ref, z_ref, wz_ref):
    @pl.when(pl.program_id(1) == 0)
    def _():
        _expand_blockdiag(re2_ref, im2_ref, wz_ref)

    z_ref[...] = jnp.dot(u_ref[...].astype(BF16), wz_ref[...], preferred_element_type=F32)


def _ssm_scan_kernel(z_ref, ap_ref, o_ref, last_ref):
    half = SLAB_STATE // 2
    n_slab = z_ref.shape[1] // SLAB_STATE

    @pl.when(pl.program_id(0) == 0)
    def _():
        last_ref[...] = jnp.zeros_like(last_ref)

    rows = lax.broadcasted_iota(jnp.int32, (SUBLANES, half), 0)

    def shift_down(x, k, fill):
        return jnp.where(rows >= k, pltpu.roll(x, k, 0), fill)

    def body(step, _):
        r0 = pl.multiple_of(step * SUBLANES, SUBLANES)
        for s in range(n_slab):
            re_sl = slice(s * SLAB_STATE, s * SLAB_STATE + half)
            im_sl = slice(s * SLAB_STATE + half, (s + 1) * SLAB_STATE)
            xr = z_ref[pl.ds(r0, SUBLANES), re_sl]
            xi = z_ref[pl.ds(r0, SUBLANES), im_sl]
            for k in (1, 2, 4):
                ar = ap_ref[k - 1:k, re_sl]
                ai = ap_ref[k - 1:k, im_sl]
                sr, si = shift_down(xr, k, 0.0), shift_down(xi, k, 0.0)
                xr, xi = xr + ar * sr - ai * si, xi + ar * si + ai * sr
            cr = jnp.broadcast_to(last_ref[SUBLANES - 1:SUBLANES, re_sl], (SUBLANES, half))
            ci = jnp.broadcast_to(last_ref[SUBLANES - 1:SUBLANES, im_sl], (SUBLANES, half))
            apr, api = ap_ref[:, re_sl], ap_ref[:, im_sl]
            xr, xi = xr + apr * cr - api * ci, xi + apr * ci + api * cr
            o_ref[pl.ds(r0, SUBLANES), re_sl] = shift_down(xr, 1, cr)
            o_ref[pl.ds(r0, SUBLANES), im_sl] = shift_down(xi, 1, ci)
            last_ref[:, re_sl] = xr
            last_ref[:, im_sl] = xi
        return 0

    lax.fori_loop(0, z_ref.shape[0] // SUBLANES, body, 0)


def _ssm_out_kernel(u_ref, kb_ref, sp_ref, re2_ref, im2_ref, d_ref, y_ref, m_ref, vt_ref):
    t_len = kb_ref.shape[0]
    tr = u_ref.shape[0]

    @pl.when(pl.program_id(1) == 0)
    def _():
        _expand_blockdiag(re2_ref, im2_ref, vt_ref)
        zero = jnp.zeros((LANES, LANES), BF16)
        for k in range(t_len):
            m_ref[k * LANES:(k + 1) * LANES, :] = jnp.concatenate(
                [zero] * k + [kb_ref[j].astype(BF16) for j in range(t_len - k)], axis=1)

    u = u_ref[...]
    y = jnp.dot(u.astype(BF16), m_ref[...], preferred_element_type=F32)
    y = y + lax.dot_general(sp_ref[...].astype(BF16), vt_ref[...], _NT, preferred_element_type=F32)
    y = jax.nn.gelu(y + d_ref[...] * u, approximate=True)
    for t in range(t_len):
        y_ref[pl.ds(t, tr, stride=t_len), :] = y[:, t * LANES:(t + 1) * LANES]


def _s5_post_kernel(y_ref, z_ref, h_ref, wg_ref, bg_ref, wo_ref, o_ref):
    y = jnp.concatenate([y_ref[s] for s in range(y_ref.shape[0])], axis=-1)
    lin = jnp.dot(y.astype(BF16), wg_ref[...], preferred_element_type=F32) + bg_ref[...]
    gated = y * jax.nn.sigmoid(lin) * jax.nn.silu(z_ref[...])
    o_ref[...] = h_ref[...] + jnp.dot(gated.astype(BF16), wo_ref[...], preferred_element_type=F32)


def _s5_derived(a_re, a_im, log_dt, b_re, b_im, c_re, c_im, d_skip):
    hp = lax.Precision.HIGHEST
    t_len = SSM_CHUNK
    g_cnt, p_cnt = a_re.shape
    n_slab = g_cnt // SLAB_GROUPS
    dt = jnp.exp(log_dt)[:, None]
    mag = jnp.exp(a_re * dt)
    ab_re, ab_im = mag * jnp.cos(a_im * dt), mag * jnp.sin(a_im * dt)
    den = a_re * a_re + a_im * a_im
    nr, ni = ab_re - 1.0, ab_im
    f_re = (nr * a_re + ni * a_im) / den
    f_im = (ni * a_re - nr * a_im) / den

    def powers(br, bi, n):
        out_r, out_i = [jnp.ones_like(br)], [jnp.zeros_like(br)]
        for _ in range(n):
            pr, pi = out_r[-1], out_i[-1]
            out_r.append(pr * br - pi * bi)
            out_i.append(pr * bi + pi * br)
        return jnp.stack(out_r), jnp.stack(out_i)

    pw_re, pw_im = powers(ab_re, ab_im, t_len)
    w_re = pw_re[:t_len] * f_re - pw_im[:t_len] * f_im
    w_im = pw_re[:t_len] * f_im + pw_im[:t_len] * f_re
    fb_re = w_re[..., None] * b_re - w_im[..., None] * b_im
    fb_im = w_re[..., None] * b_im + w_im[..., None] * b_re
    kern = (jnp.einsum('gcp,jgpd->jgcd', c_re, fb_re, precision=hp)
            - jnp.einsum('gcp,jgpd->jgcd', c_im, fb_im, precision=hp))
    eye = jnp.eye(SLAB_GROUPS, dtype=F32)
    kblk = jnp.einsum('jsgcd,gh->sjgdhc',
                      kern.reshape(t_len, n_slab, SLAB_GROUPS, SSM_GROUP, SSM_GROUP), eye)
    kblk = kblk.reshape(n_slab, t_len, LANES, LANES)

    def rows_gc(x):
        x = x.reshape(t_len, n_slab, SLAB_GROUPS, p_cnt, SSM_GROUP).transpose(1, 0, 2, 4, 3)
        x = x.reshape(n_slab, t_len * LANES, p_cnt)
        return jnp.concatenate([x, x], axis=-1)

    wz_re2, wz_im2 = rows_gc(fb_re[::-1]), rows_gc(fb_im[::-1])
    cp_re = c_re[None] * pw_re[1:, :, None, :] - c_im[None] * pw_im[1:, :, None, :]
    cp_im = c_re[None] * pw_im[1:, :, None, :] + c_im[None] * pw_re[1:, :, None, :]
    vt_re2 = rows_gc(cp_re.transpose(0, 1, 3, 2))
    vt_im2 = rows_gc(-cp_im.transpose(0, 1, 3, 2))
    ap_re, ap_im = powers(pw_re[t_len], pw_im[t_len], SUBLANES)
    a_pow = jnp.concatenate([ap_re[1:].reshape(SUBLANES, n_slab, -1),
                             ap_im[1:].reshape(SUBLANES, n_slab, -1)], axis=-1)
    a_pow = a_pow.reshape(SUBLANES, n_slab * SLAB_STATE)
    d_til = jnp.tile(d_skip.reshape(n_slab, 1, LANES), (1, 1, t_len))
    return kblk, wz_re2, wz_im2, vt_re2, vt_im2, a_pow, d_til


def _s5_layer(h, g, w_in, a_re, a_im, log_dt, b_re, b_im, c_re, c_im, d_skip, w_glu, b_glu, w_out):
    seq, d = h.shape
    n_slab = d // LANES
    t_len = SSM_CHUNK
    n_chunk = seq // t_len
    tw = t_len * LANES
    sw = SLAB_STATE
    kblk, wz_re2, wz_im2, vt_re2, vt_im2, a_pow, d_til = _s5_derived(
        a_re, a_im, log_dt, b_re, b_im, c_re, c_im, d_skip)

    tm = 512
    uc, z = pl.pallas_call(
        _s5_in_kernel,
        grid=(seq // tm,),
        in_specs=[pl.BlockSpec((tm, d), lambda i: (i, 0)),
                  pl.BlockSpec((1, d), lambda i: (0, 0)),
                  pl.BlockSpec((d, 2 * d), lambda i: (0, 0))],
        out_specs=[pl.BlockSpec((n_slab, tm // t_len, tw), lambda i: (0, i, 0)),
                   pl.BlockSpec((tm, d), lambda i: (i, 0))],
        out_shape=[jax.ShapeDtypeStruct((n_slab, n_chunk, tw), F32),
                   jax.ShapeDtypeStruct((seq, d), F32)],
        scratch_shapes=[pltpu.VMEM((n_slab, tm, LANES), F32)],
        compiler_params=_cparams("parallel"),
        name="s5_in",
    )(h, g.reshape(1, d), w_in.astype(BF16))

    tr = 512
    slab = lambda s, i: (s, 0, 0)
    zst = pl.pallas_call(
        _ssm_state_kernel,
        grid=(n_slab, n_chunk // tr),
        in_specs=[pl.BlockSpec((None, tr, tw), lambda s, i: (s, i, 0)),
                  pl.BlockSpec((None, tw, LANES), slab),
                  pl.BlockSpec((None, tw, LANES), slab)],
        out_specs=pl.BlockSpec((tr, sw), lambda s, i: (i, s)),
        out_shape=jax.ShapeDtypeStruct((n_chunk, n_slab * sw), F32),
        scratch_shapes=[pltpu.VMEM((tw, sw), BF16)],
        compiler_params=_cparams("parallel", "arbitrary"),
        name="ssm_state",
    )(uc, wz_re2, wz_im2)

    tc = 256
    sprev = pl.pallas_call(
        _ssm_scan_kernel,
        grid=(n_chunk // tc,),
        in_specs=[pl.BlockSpec((tc, n_slab * sw), lambda i: (i, 0)),
                  pl.BlockSpec((SUBLANES, n_slab * sw), lambda i: (0, 0))],
        out_specs=pl.BlockSpec((tc, n_slab * sw), lambda i: (i, 0)),
        out_shape=jax.ShapeDtypeStruct((n_chunk, n_slab * sw), F32),
        scratch_shapes=[pltpu.VMEM((SUBLANES, n_slab * sw), F32)],
        compiler_params=_cparams("arbitrary"),
        name="ssm_scan",
    )(zst, a_pow)

    yg = pl.pallas_call(
        _ssm_out_kernel,
        grid=(n_slab, n_chunk // tr),
        in_specs=[pl.BlockSpec((None, tr, tw), lambda s, i: (s, i, 0)),
                  pl.BlockSpec((None, t_len, LANES, LANES), lambda s, i: (s, 0, 0, 0)),
                  pl.BlockSpec((tr, sw), lambda s, i: (i, s)),
                  pl.BlockSpec((None, tw, LANES), slab),
                  pl.BlockSpec((None, tw, LANES), slab),
                  pl.BlockSpec((None, 1, tw), slab)],
        out_specs=pl.BlockSpec((None, tr * t_len, LANES), lambda s, i: (s, i, 0)),
        out_shape=jax.ShapeDtypeStruct((n_slab, seq, LANES), F32),
        scratch_shapes=[pltpu.VMEM((tw, tw), BF16), pltpu.VMEM((tw, sw), BF16)],
        compiler_params=_cparams("parallel", "arbitrary"),
        name="ssm_out",
    )(uc, kblk, sprev, vt_re2, vt_im2, d_til)

    return pl.pallas_call(
        _s5_post_kernel,
        grid=(seq // tm,),
        in_specs=[pl.BlockSpec((n_slab, tm, LANES), lambda i: (0, i, 0)),
                  pl.BlockSpec((tm, d), lambda i: (i, 0)),
                  pl.BlockSpec((tm, d), lambda i: (i, 0)),
                  pl.BlockSpec((d, d), lambda i: (0, 0)),
                  pl.BlockSpec((1, d), lambda i: (0, 0)),
                  pl.BlockSpec((d, d), lambda i: (0, 0))],
        out_specs=pl.BlockSpec((tm, d), lambda i: (i, 0)),
        out_shape=jax.ShapeDtypeStruct((seq, d), F32),
        compiler_params=_cparams("parallel"),
        name="s5_post",
    )(yg, z, h, w_glu.astype(BF16), b_glu.reshape(1, d), w_out.astype(BF16))


def _attn_in_kernel(h_ref, g_ref, wqkz_ref, wvt_ref, qg_ref, kg_ref, cos_ref, sa_ref, sb_ref,
                    q_ref, ka_ref, km_ref, vt_ref, z_ref):
    tm, d = h_ref.shape
    n_head = d // HEAD_DIM
    n_blk = tm // MOBA_BLOCK
    i = pl.program_id(0)
    hn = _rms(h_ref[...], g_ref[...]).astype(BF16)
    qkz = jnp.dot(hn, wqkz_ref[...], preferred_element_type=F32)
    vt = lax.dot_general(wvt_ref[...], hn, _NT, preferred_element_type=F32)
    cosf, sa, sb = cos_ref[...], sa_ref[...], sb_ref[...]

    def norm_rope(t, gain):
        t = _rms(t, gain)
        up = pltpu.roll(t, HEAD_DIM - ROT_DIM // 2, 1)
        dn = pltpu.roll(t, ROT_DIM // 2, 1)
        return t * cosf + up * sa + dn * sb

    scale = HEAD_DIM ** -0.5
    for hd in range(n_head):
        qh = norm_rope(qkz[:, hd * HEAD_DIM:(hd + 1) * HEAD_DIM], qg_ref[...])
        q_ref[hd] = qh * scale
        kh = norm_rope(qkz[:, d + hd * HEAD_DIM:d + (hd + 1) * HEAD_DIM], kg_ref[...])
        ka_ref[hd, :, :HEAD_DIM] = kh.astype(BF16)
        for b in range(n_blk):
            km_ref[b, hd:hd + 1, :] = jnp.mean(
                kh[b * MOBA_BLOCK:(b + 1) * MOBA_BLOCK], axis=0, keepdims=True)
    lane = lax.broadcasted_iota(jnp.int32, (MOBA_BLOCK, LANES), 1)
    for b in range(n_blk):
        onehot = jnp.where(lane == i * n_blk + b, NEG_BIG, 0.0).astype(BF16)
        for hd in range(n_head):
            ka_ref[hd, b * MOBA_BLOCK:(b + 1) * MOBA_BLOCK, HEAD_DIM:] = onehot
        vt_ref[b] = vt[:, b * MOBA_BLOCK:(b + 1) * MOBA_BLOCK].astype(BF16)
    z_ref[...] = qkz[:, 2 * d:]


def _moba_kernel(q_ref, k_ref, vt_ref, km_ref, o_ref, qa_ref, sa_ref, sb_ref, m_ref, l_ref, acc_ref):
    own = pl.program_id(1)
    tq, hd = q_ref.shape
    gk = KV_GROUP * MOBA_BLOCK
    q = q_ref[...]
    gates = lax.dot_general(km_ref[...], q, _NT, precision=lax.Precision.HIGHEST,
                            preferred_element_type=F32)
    nidx = lax.broadcasted_iota(jnp.int32, gates.shape, 0)
    valid = nidx < own
    g = jnp.where(valid, gates, -jnp.inf)
    picked = jnp.zeros(gates.shape, jnp.bool_)
    for _ in range(MOBA_TOPK):
        mx = jnp.max(g, axis=0, keepdims=True)
        first = jnp.min(jnp.where(g == mx, nidx, gates.shape[0]), axis=0, keepdims=True)
        hit = nidx == first
        picked = jnp.logical_or(picked, hit)
        g = jnp.where(hit, -jnp.inf, g)
    visible = jnp.logical_or(jnp.logical_and(valid, picked), nidx == own)
    qa_ref[:hd, :] = q.T.astype(BF16)
    qa_ref[hd:, :] = jnp.where(visible, 0.0, 1.0).astype(BF16)

    m_ref[...] = jnp.full(m_ref.shape, -jnp.inf, F32)
    l_ref[...] = jnp.zeros_like(l_ref)
    acc_ref[...] = jnp.zeros_like(acc_ref)

    def score(gi, dst_ref):
        kt = k_ref[pl.ds(pl.multiple_of(gi * gk, gk), gk), :]
        dst_ref[...] = jnp.dot(kt, qa_ref[...], preferred_element_type=F32)

    def absorb(src_ref, gi, causal):
        s = src_ref[...]
        if causal:
            kpos = gi * gk + lax.broadcasted_iota(jnp.int32, s.shape, 0)
            qpos = own * tq + lax.broadcasted_iota(jnp.int32, s.shape, 1)
            s = jnp.where(kpos <= qpos, s, -jnp.inf)
        m_old = m_ref[...]
        m_new = jnp.maximum(m_old, jnp.max(s, axis=0, keepdims=True))
        alpha = jnp.exp(m_old - m_new)
        p = jnp.exp(s - m_new)
        l_ref[...] = alpha * l_ref[...] + jnp.sum(p, axis=0, keepdims=True)
        pb = p.astype(BF16)
        pv = jnp.dot(vt_ref[gi * KV_GROUP], pb[:MOBA_BLOCK], preferred_element_type=F32)
        for b in range(1, KV_GROUP):
            pv = pv + jnp.dot(vt_ref[gi * KV_GROUP + b], pb[b * MOBA_BLOCK:(b + 1) * MOBA_BLOCK],
                              preferred_element_type=F32)
        acc_ref[...] = alpha * acc_ref[...] + pv
        m_ref[...] = m_new

    n_full = own // KV_GROUP
    score(0, sa_ref)

    def pair(i, carry):
        score(2 * i + 1, sb_ref)
        absorb(sa_ref, 2 * i, False)
        score(2 * i + 2, sa_ref)
        absorb(sb_ref, 2 * i + 1, False)
        return carry

    lax.fori_loop(0, n_full // 2, pair, 0)

    @pl.when(n_full % 2 == 1)
    def _():
        absorb(sa_ref, n_full - 1, False)
        score(n_full, sa_ref)

    absorb(sa_ref, n_full, True)
    o_ref[...] = (acc_ref[...] / l_ref[...]).T


def _attn_post_kernel(o_ref, z_ref, h_ref, wo_ref, out_ref):
    gated = o_ref[...] * jax.nn.silu(z_ref[...])
    out_ref[...] = h_ref[...] + jnp.dot(gated.astype(BF16), wo_ref[...], preferred_element_type=F32)


def _rope_tables(seq):
    half = ROT_DIM // 2
    inv_freq = ROPE_THETA ** (-(jnp.arange(half, dtype=F32) * 2.0) / ROT_DIM)
    ang = jnp.arange(seq).astype(F32)[:, None] * inv_freq[None, :]
    cos, sin = jnp.cos(ang), jnp.sin(ang)
    pad = HEAD_DIM - ROT_DIM
    zeros = jnp.zeros((seq, half), F32)
    cosf = jnp.concatenate([cos, cos, jnp.ones((seq, pad), F32)], axis=1)
    sa = jnp.concatenate([-sin, zeros, jnp.zeros((seq, pad), F32)], axis=1)
    sb = jnp.concatenate([zeros, sin, jnp.zeros((seq, pad), F32)], axis=1)
    return cosf, sa, sb


def _moba_layer(h, g, w_in, q_gain, k_gain, w_out, rope):
    seq, d = h.shape
    n_head = d // HEAD_DIM
    n_blk = seq // MOBA_BLOCK
    assert n_blk <= LANES and n_blk % KV_GROUP == 0 and seq % MOBA_BLOCK == 0
    cosf, sa, sb = rope
    w_qkz = jnp.concatenate([w_in[:, :2 * d], w_in[:, 3 * d:]], axis=1).astype(BF16)
    w_vt = w_in[:, 2 * d:3 * d].T.astype(BF16)

    tm = 512
    bpt = tm // MOBA_BLOCK
    row = lambda i: (i, 0)
    const = lambda i: (0, 0)
    q, kaug, kmean, vt, z = pl.pallas_call(
        _attn_in_kernel,
        grid=(seq // tm,),
        in_specs=[pl.BlockSpec((tm, d), row),
                  pl.BlockSpec((1, d), const),
                  pl.BlockSpec((d, 3 * d), const),
                  pl.BlockSpec((d, d), const),
                  pl.BlockSpec((1, HEAD_DIM), const),
                  pl.BlockSpec((1, HEAD_DIM), const),
                  pl.BlockSpec((tm, HEAD_DIM), row),
                  pl.BlockSpec((tm, HEAD_DIM), row),
                  pl.BlockSpec((tm, HEAD_DIM), row)],
        out_specs=[pl.BlockSpec((n_head, tm, HEAD_DIM), lambda i: (0, i, 0)),
                   pl.BlockSpec((n_head, tm, 2 * HEAD_DIM), lambda i: (0, i, 0)),
                   pl.BlockSpec((bpt, n_head, HEAD_DIM), lambda i: (i, 0, 0)),
                   pl.BlockSpec((bpt, d, MOBA_BLOCK), lambda i: (i, 0, 0)),
                   pl.BlockSpec((tm, d), row)],
        out_shape=[jax.ShapeDtypeStruct((n_head, seq, HEAD_DIM), F32),
                   jax.ShapeDtypeStruct((n_head, seq, 2 * HEAD_DIM), BF16),
                   jax.ShapeDtypeStruct((n_blk, n_head, HEAD_DIM), F32),
                   jax.ShapeDtypeStruct((n_blk, d, MOBA_BLOCK), BF16),
                   jax.ShapeDtypeStruct((seq, d), F32)],
        compiler_params=_cparams("parallel"),
        name="attn_in",
    )(h, g.reshape(1, d), w_qkz, w_vt, q_gain.reshape(1, HEAD_DIM), k_gain.reshape(1, HEAD_DIM),
      cosf, sa, sb)

    km = jnp.pad(kmean.transpose(1, 0, 2), ((0, 0), (0, LANES - n_blk), (0, 0)))

    o = pl.pallas_call(
        _moba_kernel,
        grid=(n_head, n_blk),
        in_specs=[pl.BlockSpec((None, MOBA_BLOCK, HEAD_DIM), lambda hd, i: (hd, i, 0)),
                  pl.BlockSpec((None, seq, 2 * HEAD_DIM), lambda hd, i: (hd, 0, 0)),
                  pl.BlockSpec((n_blk, HEAD_DIM, MOBA_BLOCK), lambda hd, i: (0, hd, 0)),
                  pl.BlockSpec((None, LANES, HEAD_DIM), lambda hd, i: (hd, 0, 0))],
        out_specs=pl.BlockSpec((MOBA_BLOCK, HEAD_DIM), lambda hd, i: (i, hd)),
        out_shape=jax.ShapeDtypeStruct((seq, d), F32),
        scratch_shapes=[pltpu.VMEM((2 * HEAD_DIM, MOBA_BLOCK), BF16),
                        pltpu.VMEM((KV_GROUP * MOBA_BLOCK, MOBA_BLOCK), F32),
                        pltpu.VMEM((KV_GROUP * MOBA_BLOCK, MOBA_BLOCK), F32),
                        pltpu.VMEM((1, MOBA_BLOCK), F32),
                        pltpu.VMEM((1, MOBA_BLOCK), F32),
                        pltpu.VMEM((HEAD_DIM, MOBA_BLOCK), F32)],
        compiler_params=_cparams("parallel", "arbitrary"),
        name="moba_attn",
    )(q, kaug, vt, km)

    return pl.pallas_call(
        _attn_post_kernel,
        grid=(seq // tm,),
        in_specs=[pl.BlockSpec((tm, d), row),
                  pl.BlockSpec((tm, d), row),
                  pl.BlockSpec((tm, d), row),
                  pl.BlockSpec((d, d), const)],
        out_specs=pl.BlockSpec((tm, d), row),
        out_shape=jax.ShapeDtypeStruct((seq, d), F32),
        compiler_params=_cparams("parallel"),
        name="attn_post",
    )(o, z, h, w_out.astype(BF16))


def kernel(x, norm_g, ssm_w_in, ssm_a_re, ssm_a_im, ssm_log_dt, ssm_b_re, ssm_b_im, ssm_c_re, ssm_c_im, ssm_d, ssm_w_glu, ssm_b_glu, ssm_w_out, attn_w_in, attn_q_gain, attn_k_gain, attn_w_out):
    bsz, seq, d = x.shape
    depth = norm_g.shape[0]
    rope = _rope_tables(seq)
    outs = []
    for b in range(bsz):
        h = x[b]
        for i in range(depth):
            j = i // 2
            if i % 2 == 0:
                h = _s5_layer(h, norm_g[i], ssm_w_in[j], ssm_a_re[j], ssm_a_im[j], ssm_log_dt[j],
                              ssm_b_re[j], ssm_b_im[j], ssm_c_re[j], ssm_c_im[j], ssm_d[j],
                              ssm_w_glu[j], ssm_b_glu[j], ssm_w_out[j])
            else:
                h = _moba_layer(h, norm_g[i], attn_w_in[j], attn_q_gain[j], attn_k_gain[j],
                                attn_w_out[j], rope)
        outs.append(h)
    return jnp.stack(outs)
```

```python
import jax
import jax.numpy as jnp
from jax import lax
from jax.experimental import pallas as pl
from jax.experimental.pallas import tpu as pltpu

F32 = jnp.float32
BF16 = jnp.bfloat16

NORM_EPS = 1e-6
SSM_GROUP = 16
SSM_STATE = 64
HEAD_DIM = 128
ROT_DIM = HEAD_DIM // 4
ROPE_THETA = 500000.0
MOBA_BLOCK = 256
MOBA_TOPK = 3

LANES = 128
SUBLANES = 8
SLAB_GROUPS = LANES // SSM_GROUP
SLAB_STATE = 2 * SLAB_GROUPS * SSM_STATE
SSM_CHUNK = 16
KV_GROUP = 4
NEG_BIG = -(2.0 ** 60)
LOG2_E = 1.4426950408889634
Q_TILE = 512
SEL_TILE = 2048
VMEM_LIMIT = 56 * 1024 * 1024

_NT = (((1,), (1,)), ((), ()))


def _cparams(*sem):
    return pltpu.CompilerParams(dimension_semantics=sem, vmem_limit_bytes=VMEM_LIMIT)


def _rms(x, g):
    ms = jnp.mean(x * x, axis=-1, keepdims=True)
    return x * lax.rsqrt(ms + NORM_EPS) * g


def _log2(n):
    assert n & (n - 1) == 0
    return n.bit_length() - 1


def _s5_in_kernel(h_ref, g_ref, w_ref, u_ref, z_ref, us_ref):
    tm, d = h_ref.shape
    n_row = tm // SSM_CHUNK
    hn = _rms(h_ref[...], g_ref[...]).astype(BF16)
    proj = jnp.dot(hn, w_ref[...], preferred_element_type=F32)
    z_ref[...] = proj[:, d:]
    for s in range(d // LANES):
        us_ref[s] = proj[:, s * LANES:(s + 1) * LANES]
        for t in range(SSM_CHUNK):
            u_ref[s, :, t * LANES:(t + 1) * LANES] = us_ref[s, pl.ds(t, n_row, stride=SSM_CHUNK), :]


def _expand_blockdiag(re2_ref, im2_ref, out_ref):
    shape = re2_ref.shape
    r = lax.broadcasted_iota(jnp.int32, shape, 0)
    lane = lax.broadcasted_iota(jnp.int32, shape, 1)
    row_group = (r >> _log2(SSM_GROUP)) & (SLAB_GROUPS - 1)
    lane_half = lane >> _log2(SSM_STATE)
    re2, im2 = re2_ref[...], im2_ref[...]
    half = SLAB_STATE // 2
    for j in range(half // LANES):
        msk = row_group == 2 * j + lane_half
        out_ref[:, j * LANES:(j + 1) * LANES] = jnp.where(msk, re2, 0.0).astype(BF16)
        out_ref[:, half + j * LANES:half + (j + 1) * LANES] = jnp.where(msk, im2, 0.0).astype(BF16)


def _ssm_state_kernel(u_ref, re2_ref, im2_ref, z_ref, wz_ref):
    @pl.when(pl.program_id(1) == 0)
    def _():
        _expand_blockdiag(re2_ref, im2_ref, wz_ref)

    z_ref[...] = jnp.dot(u_ref[...].astype(BF16), wz_ref[...], preferred_element_type=F32)


def _ssm_scan_kernel(z_ref, ap_ref, o_ref, last_ref):
    half = SLAB_STATE // 2
    n_slab = z_ref.shape[1] // SLAB_STATE

    @pl.when(pl.program_id(0) == 0)
    def _():
        last_ref[...] = jnp.zeros_like(last_ref)

    rows = lax.broadcasted_iota(jnp.int32, (SUBLANES, half), 0)

    def shift_down(x, k, fill):
        return jnp.where(rows >= k, pltpu.roll(x, k, 0), fill)

    def body(step, _):
        r0 = pl.multiple_of(step * SUBLANES, SUBLANES)
        for s in range(n_slab):
            re_sl = slice(s * SLAB_STATE, s * SLAB_STATE + half)
            im_sl = slice(s * SLAB_STATE + half, (s + 1) * SLAB_STATE)
            xr = z_ref[pl.ds(r0, SUBLANES), re_sl]
            xi = z_ref[pl.ds(r0, SUBLANES), im_sl]
            for k in (1, 2, 4):
                ar = ap_ref[k - 1:k, re_sl]
                ai = ap_ref[k - 1:k, im_sl]
                sr, si = shift_down(xr, k, 0.0), shift_down(xi, k, 0.0)
                xr, xi = xr + ar * sr - ai * si, xi + ar * si + ai * sr
            cr = jnp.broadcast_to(last_ref[SUBLANES - 1:SUBLANES, re_sl], (SUBLANES, half))
            ci = jnp.broadcast_to(last_ref[SUBLANES - 1:SUBLANES, im_sl], (SUBLANES, half))
            apr, api = ap_ref[:, re_sl], ap_ref[:, im_sl]
            xr, xi = xr + apr * cr - api * ci, xi + apr * ci + api * cr
            o_ref[pl.ds(r0, SUBLANES), re_sl] = shift_down(xr, 1, cr)
            o_ref[pl.ds(r0, SUBLANES), im_sl] = shift_down(xi, 1, ci)
            last_ref[:, re_sl] = xr
            last_ref[:, im_sl] = xi
        return 0

    lax.fori_loop(0, z_ref.shape[0] // SUBLANES, body, 0)


def _ssm_out_kernel(u_ref, kb_ref, sp_ref, re2_ref, im2_ref, d_ref, y_ref, m_ref, vt_ref):
    t_len = kb_ref.shape[0]
    tr = u_ref.shape[0]

    @pl.when(pl.program_id(1) == 0)
    def _():
        _expand_blockdiag(re2_ref, im2_ref, vt_ref)
        zero = jnp.zeros((LANES, LANES), BF16)
        for k in range(t_len):
            m_ref[k * LANES:(k + 1) * LANES, :] = jnp.concatenate(
                [zero] * k + [kb_ref[j].astype(BF16) for j in range(t_len - k)], axis=1)

    u = u_ref[...]
    y = jnp.dot(u.astype(BF16), m_ref[...], preferred_element_type=F32)
    y = y + lax.dot_general(sp_ref[...].astype(BF16), vt_ref[...], _NT, preferred_element_type=F32)
    y = jax.nn.gelu(y + d_ref[...] * u, approximate=True)
    for t in range(t_len):
        y_ref[pl.ds(t, tr, stride=t_len), :] = y[:, t * LANES:(t + 1) * LANES]


def _s5_post_kernel(y_ref, z_ref, h_ref, wg_ref, bg_ref, wo_ref, o_ref):
    y = jnp.concatenate([y_ref[s] for s in range(y_ref.shape[0])], axis=-1)
    lin = jnp.dot(y.astype(BF16), wg_ref[...], preferred_element_type=F32) + bg_ref[...]
    gated = y * jax.nn.sigmoid(lin) * jax.nn.silu(z_ref[...])
    o_ref[...] = h_ref[...] + jnp.dot(gated.astype(BF16), wo_ref[...], preferred_element_type=F32)


def _s5_derived(a_re, a_im, log_dt, b_re, b_im, c_re, c_im, d_skip):
    hp = lax.Precision.HIGHEST
    t_len = SSM_CHUNK
    g_cnt, p_cnt = a_re.shape
    n_slab = g_cnt // SLAB_GROUPS
    dt = jnp.exp(log_dt)[:, None]
    mag = jnp.exp(a_re * dt)
    ab_re, ab_im = mag * jnp.cos(a_im * dt), mag * jnp.sin(a_im * dt)
    den = a_re * a_re + a_im * a_im
    nr, ni = ab_re - 1.0, ab_im
    f_re = (nr * a_re + ni * a_im) / den
    f_im = (ni * a_re - nr * a_im) / den

    def powers(br, bi, n):
        out_r, out_i = [jnp.ones_like(br)], [jnp.zeros_like(br)]
        for _ in range(n):
            pr, pi = out_r[-1], out_i[-1]
            out_r.append(pr * br - pi * bi)
            out_i.append(pr * bi + pi * br)
        return jnp.stack(out_r), jnp.stack(out_i)

    pw_re, pw_im = powers(ab_re, ab_im, t_len)
    w_re = pw_re[:t_len] * f_re - pw_im[:t_len] * f_im
    w_im = pw_re[:t_len] * f_im + pw_im[:t_len] * f_re
    fb_re = w_re[..., None] * b_re - w_im[..., None] * b_im
    fb_im = w_re[..., None] * b_im + w_im[..., None] * b_re
    kern = (jnp.einsum('gcp,jgpd->jgcd', c_re, fb_re, precision=hp)
            - jnp.einsum('gcp,jgpd->jgcd', c_im, fb_im, precision=hp))
    eye = jnp.eye(SLAB_GROUPS, dtype=F32)
    kblk = jnp.einsum('jsgcd,gh->sjgdhc',
                      kern.reshape(t_len, n_slab, SLAB_GROUPS, SSM_GROUP, SSM_GROUP), eye)
    kblk = kblk.reshape(n_slab, t_len, LANES, LANES)

    def rows_gc(x):
        x = x.reshape(t_len, n_slab, SLAB_GROUPS, p_cnt, SSM_GROUP).transpose(1, 0, 2, 4, 3)
        x = x.reshape(n_slab, t_len * LANES, p_cnt)
        return jnp.concatenate([x, x], axis=-1)

    wz_re2, wz_im2 = rows_gc(fb_re[::-1]), rows_gc(fb_im[::-1])
    cp_re = c_re[None] * pw_re[1:, :, None, :] - c_im[None] * pw_im[1:, :, None, :]
    cp_im = c_re[None] * pw_im[1:, :, None, :] + c_im[None] * pw_re[1:, :, None, :]
    vt_re2 = rows_gc(cp_re.transpose(0, 1, 3, 2))
    vt_im2 = rows_gc(-cp_im.transpose(0, 1, 3, 2))
    ap_re, ap_im = powers(pw_re[t_len], pw_im[t_len], SUBLANES)
    a_pow = jnp.concatenate([ap_re[1:].reshape(SUBLANES, n_slab, -1),
                             ap_im[1:].reshape(SUBLANES, n_slab, -1)], axis=-1)
    a_pow = a_pow.reshape(SUBLANES, n_slab * SLAB_STATE)
    d_til = jnp.tile(d_skip.reshape(n_slab, 1, LANES), (1, 1, t_len))
    return kblk, wz_re2, wz_im2, vt_re2, vt_im2, a_pow, d_til


def _s5_layer(h, g, w_in, a_re, a_im, log_dt, b_re, b_im, c_re, c_im, d_skip, w_glu, b_glu, w_out):
    seq, d = h.shape
    n_slab = d // LANES
    t_len = SSM_CHUNK
    n_chunk = seq // t_len
    tw = t_len * LANES
    sw = SLAB_STATE
    kblk, wz_re2, wz_im2, vt_re2, vt_im2, a_pow, d_til = _s5_derived(
        a_re, a_im, log_dt, b_re, b_im, c_re, c_im, d_skip)

    tm = 512
    uc, z = pl.pallas_call(
        _s5_in_kernel,
        grid=(seq // tm,),
        in_specs=[pl.BlockSpec((tm, d), lambda i: (i, 0)),
                  pl.BlockSpec((1, d), lambda i: (0, 0)),
                  pl.BlockSpec((d, 2 * d), lambda i: (0, 0))],
        out_specs=[pl.BlockSpec((n_slab, tm // t_len, tw), lambda i: (0, i, 0)),
                   pl.BlockSpec((tm, d), lambda i: (i, 0))],
        out_shape=[jax.ShapeDtypeStruct((n_slab, n_chunk, tw), F32),
                   jax.ShapeDtypeStruct((seq, d), F32)],
        scratch_shapes=[pltpu.VMEM((n_slab, tm, LANES), F32)],
        compiler_params=_cparams("parallel"),
        name="s5_in",
    )(h, g.reshape(1, d), w_in.astype(BF16))

    tr = 512
    slab = lambda s, i: (s, 0, 0)
    zst = pl.pallas_call(
        _ssm_state_kernel,
        grid=(n_slab, n_chunk // tr),
        in_specs=[pl.BlockSpec((None, tr, tw), lambda s, i: (s, i, 0)),
                  pl.BlockSpec((None, tw, LANES), slab),
                  pl.BlockSpec((None, tw, LANES), slab)],
        out_specs=pl.BlockSpec((tr, sw), lambda s, i: (i, s)),
        out_shape=jax.ShapeDtypeStruct((n_chunk, n_slab * sw), F32),
        scratch_shapes=[pltpu.VMEM((tw, sw), BF16)],
        compiler_params=_cparams("parallel", "arbitrary"),
        name="ssm_state",
    )(uc, wz_re2, wz_im2)

    tc = 256
    sprev = pl.pallas_call(
        _ssm_scan_kernel,
        grid=(n_chunk // tc,),
        in_specs=[pl.BlockSpec((tc, n_slab * sw), lambda i: (i, 0)),
                  pl.BlockSpec((SUBLANES, n_slab * sw), lambda i: (0, 0))],
        out_specs=pl.BlockSpec((tc, n_slab * sw), lambda i: (i, 0)),
        out_shape=jax.ShapeDtypeStruct((n_chunk, n_slab * sw), F32),
        scratch_shapes=[pltpu.VMEM((SUBLANES, n_slab * sw), F32)],
        compiler_params=_cparams("arbitrary"),
        name="ssm_scan",
    )(zst, a_pow)

    yg = pl.pallas_call(
        _ssm_out_kernel,
        grid=(n_slab, n_chunk // tr),
        in_specs=[pl.BlockSpec((None, tr, tw), lambda s, i: (s, i, 0)),
                  pl.BlockSpec((None, t_len, LANES, LANES), lambda s, i: (s, 0, 0, 0)),
                  pl.BlockSpec((tr, sw), lambda s, i: (i, s)),
                  pl.BlockSpec((None, tw, LANES), slab),
                  pl.BlockSpec((None, tw, LANES), slab),
                  pl.BlockSpec((None, 1, tw), slab)],
        out_specs=pl.BlockSpec((None, tr * t_len, LANES), lambda s, i: (s, i, 0)),
        out_shape=jax.ShapeDtypeStruct((n_slab, seq, LANES), F32),
        scratch_shapes=[pltpu.VMEM((tw, tw), BF16), pltpu.VMEM((tw, sw), BF16)],
        compiler_params=_cparams("parallel", "arbitrary"),
        name="ssm_out",
    )(uc, kblk, sprev, vt_re2, vt_im2, d_til)

    return pl.pallas_call(
        _s5_post_kernel,
        grid=(seq // tm,),
        in_specs=[pl.BlockSpec((n_slab, tm, LANES), lambda i: (0, i, 0)),
                  pl.BlockSpec((tm, d), lambda i: (i, 0)),
                  pl.BlockSpec((tm, d), lambda i: (i, 0)),
                  pl.BlockSpec((d, d), lambda i: (0, 0)),
                  pl.BlockSpec((1, d), lambda i: (0, 0)),
                  pl.BlockSpec((d, d), lambda i: (0, 0))],
        out_specs=pl.BlockSpec((tm, d), lambda i: (i, 0)),
        out_shape=jax.ShapeDtypeStruct((seq, d), F32),
        compiler_params=_cparams("parallel"),
        name="s5_post",
    )(yg, z, h, w_glu.astype(BF16), b_glu.reshape(1, d), w_out.astype(BF16))


def _attn_in_kernel(h_ref, g_ref, wqkz_ref, wvt_ref, qg_ref, kg_ref, cos_ref, sa_ref, sb_ref,
                    q_ref, ka_ref, km_ref, vt_ref, z_ref):
    tm, d = h_ref.shape
    n_head = d // HEAD_DIM
    n_blk = tm // MOBA_BLOCK
    i = pl.program_id(0)
    hn = _rms(h_ref[...], g_ref[...]).astype(BF16)
    qkz = jnp.dot(hn, wqkz_ref[...], preferred_element_type=F32)
    vt = lax.dot_general(wvt_ref[...], hn, _NT, preferred_element_type=F32)
    cosf, sa, sb = cos_ref[...], sa_ref[...], sb_ref[...]

    def norm_rope(t, gain):
        t = _rms(t, gain)
        up = pltpu.roll(t, HEAD_DIM - ROT_DIM // 2, 1)
        dn = pltpu.roll(t, ROT_DIM // 2, 1)
        return t * cosf + up * sa + dn * sb

    scale = HEAD_DIM ** -0.5 * LOG2_E
    for hd in range(n_head):
        qh = norm_rope(qkz[:, hd * HEAD_DIM:(hd + 1) * HEAD_DIM], qg_ref[...])
        q_ref[hd] = qh * scale
        kh = norm_rope(qkz[:, d + hd * HEAD_DIM:d + (hd + 1) * HEAD_DIM], kg_ref[...])
        ka_ref[hd, :, :HEAD_DIM] = kh.astype(BF16)
        for b in range(n_blk):
            km_ref[b, hd:hd + 1, :] = jnp.mean(
                kh[b * MOBA_BLOCK:(b + 1) * MOBA_BLOCK], axis=0, keepdims=True)
    lane = lax.broadcasted_iota(jnp.int32, (MOBA_BLOCK, LANES), 1)
    for b in range(n_blk):
        onehot = jnp.where(lane == i * n_blk + b, NEG_BIG, 0.0).astype(BF16)
        for hd in range(n_head):
            ka_ref[hd, b * MOBA_BLOCK:(b + 1) * MOBA_BLOCK, HEAD_DIM:] = onehot
        vt_ref[b] = vt[:, b * MOBA_BLOCK:(b + 1) * MOBA_BLOCK].astype(BF16)
    z_ref[...] = qkz[:, 2 * d:]


def _moba_select_kernel(q_ref, km_ref, qa_ref):
    ts, hd = q_ref.shape
    n_blk = km_ref.shape[0]
    q = q_ref[...]
    gates = lax.dot_general(km_ref[...], q, _NT, precision=lax.Precision.HIGHEST,
                            preferred_element_type=F32)
    nidx = lax.broadcasted_iota(jnp.int32, gates.shape, 0)
    qpos = pl.program_id(1) * ts + lax.broadcasted_iota(jnp.int32, gates.shape, 1)
    own = qpos >> _log2(MOBA_BLOCK)
    valid = nidx < own
    g = jnp.where(valid, gates, -jnp.inf)
    picked = jnp.zeros(gates.shape, jnp.bool_)
    for _ in range(MOBA_TOPK):
        mx = jnp.max(g, axis=0, keepdims=True)
        first = jnp.min(jnp.where(g == mx, nidx, n_blk), axis=0, keepdims=True)
        hit = nidx == first
        picked = jnp.logical_or(picked, hit)
        g = jnp.where(hit, -jnp.inf, g)
    visible = jnp.logical_or(jnp.logical_and(valid, picked), nidx == own)
    qa_ref[:hd, :] = q.T.astype(BF16)
    qa_ref[hd:hd + n_blk, :] = jnp.where(visible, 0.0, 1.0).astype(BF16)
    qa_ref[hd + n_blk:, :] = jnp.zeros((qa_ref.shape[0] - hd - n_blk, ts), BF16)


def _moba_kernel(qa_ref, k_ref, vt_ref, o_ref, sa_ref, sb_ref, m_ref, l_ref, acc_ref):
    step = pl.program_id(1)
    tq = qa_ref.shape[1]
    gk = KV_GROUP * MOBA_BLOCK
    assert KV_GROUP % (tq // MOBA_BLOCK) == 0

    m_ref[...] = jnp.full(m_ref.shape, -jnp.inf, F32)
    l_ref[...] = jnp.zeros_like(l_ref)
    acc_ref[...] = jnp.zeros_like(acc_ref)

    def score(gi, dst_ref):
        kt = k_ref[pl.ds(pl.multiple_of(gi * gk, gk), gk), :]
        dst_ref[...] = jnp.dot(kt, qa_ref[...], preferred_element_type=F32)

    def absorb(src_ref, gi, causal):
        s = src_ref[...]
        if causal:
            kpos = gi * gk + lax.broadcasted_iota(jnp.int32, s.shape, 0)
            qpos = step * tq + lax.broadcasted_iota(jnp.int32, s.shape, 1)
            s = jnp.where(kpos <= qpos, s, -jnp.inf)
        m_old = m_ref[...]
        m_new = jnp.maximum(m_old, jnp.max(s, axis=0, keepdims=True))
        alpha = jnp.exp2(m_old - m_new)
        p = jnp.exp2(s - m_new)
        l_ref[...] = alpha * l_ref[...] + jnp.sum(p, axis=0, keepdims=True)
        pb = p.astype(BF16)
        pv = jnp.dot(vt_ref[gi * KV_GROUP], pb[:MOBA_BLOCK], preferred_element_type=F32)
        for b in range(1, KV_GROUP):
            pv = pv + jnp.dot(vt_ref[gi * KV_GROUP + b], pb[b * MOBA_BLOCK:(b + 1) * MOBA_BLOCK],
                              preferred_element_type=F32)
        acc_ref[...] = alpha * acc_ref[...] + pv
        m_ref[...] = m_new

    n_full = (step * (tq // MOBA_BLOCK)) // KV_GROUP
    score(0, sa_ref)

    def pair(i, carry):
        score(2 * i + 1, sb_ref)
        absorb(sa_ref, 2 * i, False)
        score(2 * i + 2, sa_ref)
        absorb(sb_ref, 2 * i + 1, False)
        return carry

    lax.fori_loop(0, n_full // 2, pair, 0)

    @pl.when(n_full % 2 == 1)
    def _():
        absorb(sa_ref, n_full - 1, False)
        score(n_full, sa_ref)

    absorb(sa_ref, n_full, True)
    o_ref[...] = (acc_ref[...] / l_ref[...]).T


def _attn_post_kernel(o_ref, z_ref, h_ref, wo_ref, out_ref):
    gated = o_ref[...] * jax.nn.silu(z_ref[...])
    out_ref[...] = h_ref[...] + jnp.dot(gated.astype(BF16), wo_ref[...], preferred_element_type=F32)


def _rope_tables(seq):
    half = ROT_DIM // 2
    inv_freq = ROPE_THETA ** (-(jnp.arange(half, dtype=F32) * 2.0) / ROT_DIM)
    ang = jnp.arange(seq).astype(F32)[:, None] * inv_freq[None, :]
    cos, sin = jnp.cos(ang), jnp.sin(ang)
    pad = HEAD_DIM - ROT_DIM
    zeros = jnp.zeros((seq, half), F32)
    cosf = jnp.concatenate([cos, cos, jnp.ones((seq, pad), F32)], axis=1)
    sa = jnp.concatenate([-sin, zeros, jnp.zeros((seq, pad), F32)], axis=1)
    sb = jnp.concatenate([zeros, sin, jnp.zeros((seq, pad), F32)], axis=1)
    return cosf, sa, sb


def _moba_layer(h, g, w_in, q_gain, k_gain, w_out, rope):
    seq, d = h.shape
    n_head = d // HEAD_DIM
    n_blk = seq // MOBA_BLOCK
    assert n_blk <= LANES and n_blk % KV_GROUP == 0 and seq % MOBA_BLOCK == 0
    cosf, sa, sb = rope
    w_qkz = jnp.concatenate([w_in[:, :2 * d], w_in[:, 3 * d:]], axis=1).astype(BF16)
    w_vt = w_in[:, 2 * d:3 * d].T.astype(BF16)

    tm = 512
    bpt = tm // MOBA_BLOCK
    row = lambda i: (i, 0)
    const = lambda i: (0, 0)
    q, kaug, kmean, vt, z = pl.pallas_call(
        _attn_in_kernel,
        grid=(seq // tm,),
        in_specs=[pl.BlockSpec((tm, d), row),
                  pl.BlockSpec((1, d), const),
                  pl.BlockSpec((d, 3 * d), const),
                  pl.BlockSpec((d, d), const),
                  pl.BlockSpec((1, HEAD_DIM), const),
                  pl.BlockSpec((1, HEAD_DIM), const),
                  pl.BlockSpec((tm, HEAD_DIM), row),
                  pl.BlockSpec((tm, HEAD_DIM), row),
                  pl.BlockSpec((tm, HEAD_DIM), row)],
        out_specs=[pl.BlockSpec((n_head, tm, HEAD_DIM), lambda i: (0, i, 0)),
                   pl.BlockSpec((n_head, tm, 2 * HEAD_DIM), lambda i: (0, i, 0)),
                   pl.BlockSpec((bpt, n_head, HEAD_DIM), lambda i: (i, 0, 0)),
                   pl.BlockSpec((bpt, d, MOBA_BLOCK), lambda i: (i, 0, 0)),
                   pl.BlockSpec((tm, d), row)],
        out_shape=[jax.ShapeDtypeStruct((n_head, seq, HEAD_DIM), F32),
                   jax.ShapeDtypeStruct((n_head, seq, 2 * HEAD_DIM), BF16),
                   jax.ShapeDtypeStruct((n_blk, n_head, HEAD_DIM), F32),
                   jax.ShapeDtypeStruct((n_blk, d, MOBA_BLOCK), BF16),
                   jax.ShapeDtypeStruct((seq, d), F32)],
        compiler_params=_cparams("parallel"),
        name="attn_in",
    )(h, g.reshape(1, d), w_qkz, w_vt, q_gain.reshape(1, HEAD_DIM), k_gain.reshape(1, HEAD_DIM),
      cosf, sa, sb)

    qa = pl.pallas_call(
        _moba_select_kernel,
        grid=(n_head, seq // SEL_TILE),
        in_specs=[pl.BlockSpec((None, SEL_TILE, HEAD_DIM), lambda hd, i: (hd, i, 0)),
                  pl.BlockSpec((None, n_blk, HEAD_DIM), lambda hd, i: (hd, 0, 0))],
        out_specs=pl.BlockSpec((None, 2 * HEAD_DIM, SEL_TILE), lambda hd, i: (hd, 0, i)),
        out_shape=jax.ShapeDtypeStruct((n_head, 2 * HEAD_DIM, seq), BF16),
        compiler_params=_cparams("parallel", "parallel"),
        name="moba_select",
    )(q, kmean.transpose(1, 0, 2))

    o = pl.pallas_call(
        _moba_kernel,
        grid=(n_head, seq // Q_TILE),
        in_specs=[pl.BlockSpec((None, 2 * HEAD_DIM, Q_TILE), lambda hd, i: (hd, 0, i)),
                  pl.BlockSpec((None, seq, 2 * HEAD_DIM), lambda hd, i: (hd, 0, 0)),
                  pl.BlockSpec((n_blk, HEAD_DIM, MOBA_BLOCK), lambda hd, i: (0, hd, 0))],
        out_specs=pl.BlockSpec((Q_TILE, HEAD_DIM), lambda hd, i: (i, hd)),
        out_shape=jax.ShapeDtypeStruct((seq, d), F32),
        scratch_shapes=[pltpu.VMEM((KV_GROUP * MOBA_BLOCK, Q_TILE), F32),
                        pltpu.VMEM((KV_GROUP * MOBA_BLOCK, Q_TILE), F32),
                        pltpu.VMEM((1, Q_TILE), F32),
                        pltpu.VMEM((1, Q_TILE), F32),
                        pltpu.VMEM((HEAD_DIM, Q_TILE), F32)],
        compiler_params=_cparams("parallel", "arbitrary"),
        name="moba_attn",
    )(qa, kaug, vt)

    return pl.pallas_call(
        _attn_post_kernel,
        grid=(seq // tm,),
        in_specs=[pl.BlockSpec((tm, d), row),
                  pl.BlockSpec((tm, d), row),
                  pl.BlockSpec((tm, d), row),
                  pl.BlockSpec((d, d), const)],
        out_specs=pl.BlockSpec((tm, d), row),
        out_shape=jax.ShapeDtypeStruct((seq, d), F32),
        compiler_params=_cparams("parallel"),
        name="attn_post",
    )(o, z, h, w_out.astype(BF16))


def kernel(x, norm_g, ssm_w_in, ssm_a_re, ssm_a_im, ssm_log_dt, ssm_b_re, ssm_b_im, ssm_c_re, ssm_c_im, ssm_d, ssm_w_glu, ssm_b_glu, ssm_w_out, attn_w_in, attn_q_gain, attn_k_gain, attn_w_out):
    bsz, seq, d = x.shape
    depth = norm_g.shape[0]
    rope = _rope_tables(seq)
    outs = []
    for b in range(bsz):
        h = x[b]
        for i in range(depth):
            j = i // 2
            if i % 2 == 0:
                h = _s5_layer(h, norm_g[i], ssm_w_in[j], ssm_a_re[j], ssm_a_im[j], ssm_log_dt[j],
                              ssm_b_re[j], ssm_b_im[j], ssm_c_re[j], ssm_c_im[j], ssm_d[j],
                              ssm_w_glu[j], ssm_b_glu[j], ssm_w_out[j])
            else:
                h = _moba_layer(h, norm_g[i], attn_w_in[j], attn_q_gain[j], attn_k_gain[j],
                                attn_w_out[j], rope)
        outs.append(h)
    return jnp.stack(outs)
```

```python
import jax
import jax.numpy as jnp
import numpy as np
from jax import lax
from jax.experimental import pallas as pl
from jax.experimental.pallas import tpu as pltpu

F32 = jnp.float32
BF16 = jnp.bfloat16

NORM_EPS = 1e-6
SSM_GROUP = 16
SSM_STATE = 64
HEAD_DIM = 128
ROT_DIM = HEAD_DIM // 4
ROPE_THETA = 500000.0
MOBA_BLOCK = 256
MOBA_TOPK = 3

LANES = 128
SUBLANES = 8
SLAB_GROUPS = LANES // SSM_GROUP
SLAB_STATE = 2 * SLAB_GROUPS * SSM_STATE
SSM_CHUNK = 16
KV_GROUP = 4
NEG_BIG = -(2.0 ** 60)
LOG2_E = 1.4426950408889634
Q_TILE = 512
SEL_TILE = 2048
VMEM_LIMIT = 56 * 1024 * 1024

_NT = (((1,), (1,)), ((), ()))


def _cparams(*sem):
    return pltpu.CompilerParams(dimension_semantics=sem, vmem_limit_bytes=VMEM_LIMIT)


def _rms(x, g):
    ms = jnp.mean(x * x, axis=-1, keepdims=True)
    return x * lax.rsqrt(ms + NORM_EPS) * g


def _log2(n):
    assert n & (n - 1) == 0
    return n.bit_length() - 1


def _s5_in_kernel(h_ref, g_ref, w_ref, u_ref, z_ref, us_ref):
    tm, d = h_ref.shape
    n_row = tm // SSM_CHUNK
    hn = _rms(h_ref[...], g_ref[...]).astype(BF16)
    proj = jnp.dot(hn, w_ref[...], preferred_element_type=F32)
    z_ref[...] = proj[:, d:]
    for s in range(d // LANES):
        us_ref[s] = proj[:, s * LANES:(s + 1) * LANES]
        for t in range(SSM_CHUNK):
            u_ref[s, :, t * LANES:(t + 1) * LANES] = us_ref[s, pl.ds(t, n_row, stride=SSM_CHUNK), :]


def _expand_blockdiag(re2_ref, im2_ref, out_ref):
    shape = re2_ref.shape
    r = lax.broadcasted_iota(jnp.int32, shape, 0)
    lane = lax.broadcasted_iota(jnp.int32, shape, 1)
    row_group = (r >> _log2(SSM_GROUP)) & (SLAB_GROUPS - 1)
    lane_half = lane >> _log2(SSM_STATE)
    re2, im2 = re2_ref[...], im2_ref[...]
    half = SLAB_STATE // 2
    for j in range(half // LANES):
        msk = row_group == 2 * j + lane_half
        out_ref[:, j * LANES:(j + 1) * LANES] = jnp.where(msk, re2, 0.0).astype(BF16)
        out_ref[:, half + j * LANES:half + (j + 1) * LANES] = jnp.where(msk, im2, 0.0).astype(BF16)


def _ssm_state_kernel(u_ref, re2_ref, im2_ref, z_ref, wz_ref):
    @pl.when(pl.program_id(1) == 0)
    def _():
        _expand_blockdiag(re2_ref, im2_ref, wz_ref)

    z_ref[...] = jnp.dot(u_ref[...].astype(BF16), wz_ref[...], preferred_element_type=F32)


def _ssm_scan_kernel(z_ref, ap_ref, o_ref, last_ref):
    half = SLAB_STATE // 2
    n_slab = z_ref.shape[1] // SLAB_STATE

    @pl.when(pl.program_id(0) == 0)
    def _():
        last_ref[...] = jnp.zeros_like(last_ref)

    rows = lax.broadcasted_iota(jnp.int32, (SUBLANES, half), 0)

    def shift_down(x, k, fill):
        return jnp.where(rows >= k, pltpu.roll(x, k, 0), fill)

    def body(step, _):
        r0 = pl.multiple_of(step * SUBLANES, SUBLANES)
        for s in range(n_slab):
            re_sl = slice(s * SLAB_STATE, s * SLAB_STATE + half)
            im_sl = slice(s * SLAB_STATE + half, (s + 1) * SLAB_STATE)
            xr = z_ref[pl.ds(r0, SUBLANES), re_sl]
            xi = z_ref[pl.ds(r0, SUBLANES), im_sl]
            for k in (1, 2, 4):
                ar = ap_ref[k - 1:k, re_sl]
                ai = ap_ref[k - 1:k, im_sl]
                sr, si = shift_down(xr, k, 0.0), shift_down(xi, k, 0.0)
                xr, xi = xr + ar * sr - ai * si, xi + ar * si + ai * sr
            cr = jnp.broadcast_to(last_ref[SUBLANES - 1:SUBLANES, re_sl], (SUBLANES, half))
            ci = jnp.broadcast_to(last_ref[SUBLANES - 1:SUBLANES, im_sl], (SUBLANES, half))
            apr, api = ap_ref[:, re_sl], ap_ref[:, im_sl]
            xr, xi = xr + apr * cr - api * ci, xi + apr * ci + api * cr
            o_ref[pl.ds(r0, SUBLANES), re_sl] = shift_down(xr, 1, cr)
            o_ref[pl.ds(r0, SUBLANES), im_sl] = shift_down(xi, 1, ci)
            last_ref[:, re_sl] = xr
            last_ref[:, im_sl] = xi
        return 0

    lax.fori_loop(0, z_ref.shape[0] // SUBLANES, body, 0)


def _ssm_out_kernel(u_ref, kb_ref, sp_ref, re2_ref, im2_ref, d_ref, y_ref, m_ref, vt_ref):
    t_len = kb_ref.shape[0]
    tr = u_ref.shape[0]

    @pl.when(pl.program_id(1) == 0)
    def _():
        _expand_blockdiag(re2_ref, im2_ref, vt_ref)
        zero = jnp.zeros((LANES, LANES), BF16)
        for k in range(t_len):
            m_ref[k * LANES:(k + 1) * LANES, :] = jnp.concatenate(
                [zero] * k + [kb_ref[j].astype(BF16) for j in range(t_len - k)], axis=1)

    u = u_ref[...]
    y = jnp.dot(u.astype(BF16), m_ref[...], preferred_element_type=F32)
    y = y + lax.dot_general(sp_ref[...].astype(BF16), vt_ref[...], _NT, preferred_element_type=F32)
    y = jax.nn.gelu(y + d_ref[...] * u, approximate=True)
    for t in range(t_len):
        y_ref[pl.ds(t, tr, stride=t_len), :] = y[:, t * LANES:(t + 1) * LANES]


def _s5_post_kernel(y_ref, z_ref, h_ref, wg_ref, bg_ref, wo_ref, o_ref):
    y = jnp.concatenate([y_ref[s] for s in range(y_ref.shape[0])], axis=-1)
    lin = jnp.dot(y.astype(BF16), wg_ref[...], preferred_element_type=F32) + bg_ref[...]
    gated = y * jax.nn.sigmoid(lin) * jax.nn.silu(z_ref[...])
    o_ref[...] = h_ref[...] + jnp.dot(gated.astype(BF16), wo_ref[...], preferred_element_type=F32)


def _s5_derived(a_re, a_im, log_dt, b_re, b_im, c_re, c_im, d_skip):
    hp = lax.Precision.HIGHEST
    t_len = SSM_CHUNK
    g_cnt, p_cnt = a_re.shape
    n_slab = g_cnt // SLAB_GROUPS
    dt = jnp.exp(log_dt)[:, None]
    mag = jnp.exp(a_re * dt)
    ab_re, ab_im = mag * jnp.cos(a_im * dt), mag * jnp.sin(a_im * dt)
    den = a_re * a_re + a_im * a_im
    nr, ni = ab_re - 1.0, ab_im
    f_re = (nr * a_re + ni * a_im) / den
    f_im = (ni * a_re - nr * a_im) / den

    def powers(exps):
        j = exps.astype(F32)[:, None, None]
        mag_j = jnp.exp(j * (a_re * dt))
        ang_j = j * (a_im * dt)
        return mag_j * jnp.cos(ang_j), mag_j * jnp.sin(ang_j)

    pw_re, pw_im = powers(jnp.arange(t_len + 1))
    w_re = pw_re[:t_len] * f_re - pw_im[:t_len] * f_im
    w_im = pw_re[:t_len] * f_im + pw_im[:t_len] * f_re
    fb_re = w_re[..., None] * b_re - w_im[..., None] * b_im
    fb_im = w_re[..., None] * b_im + w_im[..., None] * b_re
    kern = (jnp.einsum('gcp,jgpd->jgcd', c_re, fb_re, precision=hp)
            - jnp.einsum('gcp,jgpd->jgcd', c_im, fb_im, precision=hp))
    eye = jnp.eye(SLAB_GROUPS, dtype=F32)
    kblk = jnp.einsum('jsgcd,gh->sjgdhc',
                      kern.reshape(t_len, n_slab, SLAB_GROUPS, SSM_GROUP, SSM_GROUP), eye)
    kblk = kblk.reshape(n_slab, t_len, LANES, LANES)

    def rows_gc(x):
        x = x.reshape(t_len, n_slab, SLAB_GROUPS, p_cnt, SSM_GROUP).transpose(1, 0, 2, 4, 3)
        x = x.reshape(n_slab, t_len * LANES, p_cnt)
        return jnp.concatenate([x, x], axis=-1)

    wz_re2, wz_im2 = rows_gc(fb_re[::-1]), rows_gc(fb_im[::-1])
    cp_re = c_re[None] * pw_re[1:, :, None, :] - c_im[None] * pw_im[1:, :, None, :]
    cp_im = c_re[None] * pw_im[1:, :, None, :] + c_im[None] * pw_re[1:, :, None, :]
    vt_re2 = rows_gc(cp_re.transpose(0, 1, 3, 2))
    vt_im2 = rows_gc(-cp_im.transpose(0, 1, 3, 2))
    ap_re, ap_im = powers(t_len * jnp.arange(1, SUBLANES + 1))
    a_pow = jnp.concatenate([ap_re.reshape(SUBLANES, n_slab, -1),
                             ap_im.reshape(SUBLANES, n_slab, -1)], axis=-1)
    a_pow = a_pow.reshape(SUBLANES, n_slab * SLAB_STATE)
    d_til = jnp.tile(d_skip.reshape(n_slab, 1, LANES), (1, 1, t_len))
    return kblk, wz_re2, wz_im2, vt_re2, vt_im2, a_pow, d_til


def _s5_layer(h, g, w_in, a_re, a_im, log_dt, b_re, b_im, c_re, c_im, d_skip, w_glu, b_glu, w_out):
    seq, d = h.shape
    n_slab = d // LANES
    t_len = SSM_CHUNK
    n_chunk = seq // t_len
    tw = t_len * LANES
    sw = SLAB_STATE
    kblk, wz_re2, wz_im2, vt_re2, vt_im2, a_pow, d_til = _s5_derived(
        a_re, a_im, log_dt, b_re, b_im, c_re, c_im, d_skip)

    tm = 512
    uc, z = pl.pallas_call(
        _s5_in_kernel,
        grid=(seq // tm,),
        in_specs=[pl.BlockSpec((tm, d), lambda i: (i, 0)),
                  pl.BlockSpec((1, d), lambda i: (0, 0)),
                  pl.BlockSpec((d, 2 * d), lambda i: (0, 0))],
        out_specs=[pl.BlockSpec((n_slab, tm // t_len, tw), lambda i: (0, i, 0)),
                   pl.BlockSpec((tm, d), lambda i: (i, 0))],
        out_shape=[jax.ShapeDtypeStruct((n_slab, n_chunk, tw), F32),
                   jax.ShapeDtypeStruct((seq, d), F32)],
        scratch_shapes=[pltpu.VMEM((n_slab, tm, LANES), F32)],
        compiler_params=_cparams("parallel"),
        name="s5_in",
    )(h, g.reshape(1, d), w_in.astype(BF16))

    tr = 512
    slab = lambda s, i: (s, 0, 0)
    zst = pl.pallas_call(
        _ssm_state_kernel,
        grid=(n_slab, n_chunk // tr),
        in_specs=[pl.BlockSpec((None, tr, tw), lambda s, i: (s, i, 0)),
                  pl.BlockSpec((None, tw, LANES), slab),
                  pl.BlockSpec((None, tw, LANES), slab)],
        out_specs=pl.BlockSpec((tr, sw), lambda s, i: (i, s)),
        out_shape=jax.ShapeDtypeStruct((n_chunk, n_slab * sw), F32),
        scratch_shapes=[pltpu.VMEM((tw, sw), BF16)],
        compiler_params=_cparams("parallel", "arbitrary"),
        name="ssm_state",
    )(uc, wz_re2, wz_im2)

    tc = 256
    sprev = pl.pallas_call(
        _ssm_scan_kernel,
        grid=(n_chunk // tc,),
        in_specs=[pl.BlockSpec((tc, n_slab * sw), lambda i: (i, 0)),
                  pl.BlockSpec((SUBLANES, n_slab * sw), lambda i: (0, 0))],
        out_specs=pl.BlockSpec((tc, n_slab * sw), lambda i: (i, 0)),
        out_shape=jax.ShapeDtypeStruct((n_chunk, n_slab * sw), F32),
        scratch_shapes=[pltpu.VMEM((SUBLANES, n_slab * sw), F32)],
        compiler_params=_cparams("arbitrary"),
        name="ssm_scan",
    )(zst, a_pow)

    yg = pl.pallas_call(
        _ssm_out_kernel,
        grid=(n_slab, n_chunk // tr),
        in_specs=[pl.BlockSpec((None, tr, tw), lambda s, i: (s, i, 0)),
                  pl.BlockSpec((None, t_len, LANES, LANES), lambda s, i: (s, 0, 0, 0)),
                  pl.BlockSpec((tr, sw), lambda s, i: (i, s)),
                  pl.BlockSpec((None, tw, LANES), slab),
                  pl.BlockSpec((None, tw, LANES), slab),
                  pl.BlockSpec((None, 1, tw), slab)],
        out_specs=pl.BlockSpec((None, tr * t_len, LANES), lambda s, i: (s, i, 0)),
        out_shape=jax.ShapeDtypeStruct((n_slab, seq, LANES), F32),
        scratch_shapes=[pltpu.VMEM((tw, tw), BF16), pltpu.VMEM((tw, sw), BF16)],
        compiler_params=_cparams("parallel", "arbitrary"),
        name="ssm_out",
    )(uc, kblk, sprev, vt_re2, vt_im2, d_til)

    return pl.pallas_call(
        _s5_post_kernel,
        grid=(seq // tm,),
        in_specs=[pl.BlockSpec((n_slab, tm, LANES), lambda i: (0, i, 0)),
                  pl.BlockSpec((tm, d), lambda i: (i, 0)),
                  pl.BlockSpec((tm, d), lambda i: (i, 0)),
                  pl.BlockSpec((d, d), lambda i: (0, 0)),
                  pl.BlockSpec((1, d), lambda i: (0, 0)),
                  pl.BlockSpec((d, d), lambda i: (0, 0))],
        out_specs=pl.BlockSpec((tm, d), lambda i: (i, 0)),
        out_shape=jax.ShapeDtypeStruct((seq, d), F32),
        compiler_params=_cparams("parallel"),
        name="s5_post",
    )(yg, z, h, w_glu.astype(BF16), b_glu.reshape(1, d), w_out.astype(BF16))


def _attn_in_kernel(h_ref, g_ref, wqkz_ref, wvt_ref, qg_ref, kg_ref, cos_ref, sa_ref, sb_ref,
                    q_ref, ka_ref, km_ref, vt_ref, z_ref):
    tm, d = h_ref.shape
    n_head = d // HEAD_DIM
    n_blk = tm // MOBA_BLOCK
    i = pl.program_id(0)
    hn = _rms(h_ref[...], g_ref[...]).astype(BF16)
    qkz = jnp.dot(hn, wqkz_ref[...], preferred_element_type=F32)
    vt = lax.dot_general(wvt_ref[...], hn, _NT, preferred_element_type=F32)
    cosf, sa, sb = cos_ref[...], sa_ref[...], sb_ref[...]

    def norm_rope(t, gain):
        t = _rms(t, gain)
        up = pltpu.roll(t, HEAD_DIM - ROT_DIM // 2, 1)
        dn = pltpu.roll(t, ROT_DIM // 2, 1)
        return t * cosf + up * sa + dn * sb

    scale = HEAD_DIM ** -0.5 * LOG2_E
    for hd in range(n_head):
        qh = norm_rope(qkz[:, hd * HEAD_DIM:(hd + 1) * HEAD_DIM], qg_ref[...])
        q_ref[hd] = qh * scale
        kh = norm_rope(qkz[:, d + hd * HEAD_DIM:d + (hd + 1) * HEAD_DIM], kg_ref[...])
        ka_ref[hd, :, :HEAD_DIM] = kh.astype(BF16)
        for b in range(n_blk):
            km_ref[b, hd:hd + 1, :] = jnp.mean(
                kh[b * MOBA_BLOCK:(b + 1) * MOBA_BLOCK], axis=0, keepdims=True)
    lane = lax.broadcasted_iota(jnp.int32, (MOBA_BLOCK, LANES), 1)
    for b in range(n_blk):
        onehot = jnp.where(lane == i * n_blk + b, NEG_BIG, 0.0).astype(BF16)
        for hd in range(n_head):
            ka_ref[hd, b * MOBA_BLOCK:(b + 1) * MOBA_BLOCK, HEAD_DIM:] = onehot
        vt_ref[b] = vt[:, b * MOBA_BLOCK:(b + 1) * MOBA_BLOCK].astype(BF16)
    z_ref[...] = qkz[:, 2 * d:]


def _moba_select_kernel(q_ref, km_ref, qa_ref):
    ts, hd = q_ref.shape
    n_blk = km_ref.shape[0]
    q = q_ref[...]
    gates = lax.dot_general(km_ref[...], q, _NT, precision=lax.Precision.HIGHEST,
                            preferred_element_type=F32)
    nidx = lax.broadcasted_iota(jnp.int32, gates.shape, 0)
    qpos = pl.program_id(1) * ts + lax.broadcasted_iota(jnp.int32, gates.shape, 1)
    own = qpos >> _log2(MOBA_BLOCK)
    valid = nidx < own
    g = jnp.where(valid, gates, -jnp.inf)
    picked = jnp.zeros(gates.shape, jnp.bool_)
    for _ in range(MOBA_TOPK):
        mx = jnp.max(g, axis=0, keepdims=True)
        first = jnp.min(jnp.where(g == mx, nidx, n_blk), axis=0, keepdims=True)
        hit = nidx == first
        picked = jnp.logical_or(picked, hit)
        g = jnp.where(hit, -jnp.inf, g)
    visible = jnp.logical_or(jnp.logical_and(valid, picked), nidx == own)
    qa_ref[:hd, :] = q.T.astype(BF16)
    qa_ref[hd:hd + n_blk, :] = jnp.where(visible, 0.0, 1.0).astype(BF16)
    qa_ref[hd + n_blk:, :] = jnp.zeros((qa_ref.shape[0] - hd - n_blk, ts), BF16)


def _moba_kernel(qa_ref, k_ref, vt_ref, o_ref, sa_ref, sb_ref, m_ref, l_ref, acc_ref):
    step = pl.program_id(1)
    tq = qa_ref.shape[1]
    gk = KV_GROUP * MOBA_BLOCK
    assert KV_GROUP % (tq // MOBA_BLOCK) == 0

    m_ref[...] = jnp.full(m_ref.shape, -jnp.inf, F32)
    l_ref[...] = jnp.zeros_like(l_ref)
    acc_ref[...] = jnp.zeros_like(acc_ref)

    def score(gi, dst_ref):
        kt = k_ref[pl.ds(pl.multiple_of(gi * gk, gk), gk), :]
        dst_ref[...] = jnp.dot(kt, qa_ref[...], preferred_element_type=F32)

    def rows(b):
        return slice(b * MOBA_BLOCK, (b + 1) * MOBA_BLOCK)

    def absorb(src_ref, gi, causal):
        if causal:
            kpos = gi * gk + lax.broadcasted_iota(jnp.int32, src_ref.shape, 0)
            qpos = step * tq + lax.broadcasted_iota(jnp.int32, src_ref.shape, 1)
            src_ref[...] = jnp.where(kpos <= qpos, src_ref[...], -jnp.inf)
        m_old = m_ref[...]
        m_new = m_old
        for b in range(KV_GROUP):
            m_new = jnp.maximum(m_new, jnp.max(src_ref[rows(b), :], axis=0, keepdims=True))
        alpha = jnp.exp2(m_old - m_new)
        lsum = alpha * l_ref[...]
        pv = alpha * acc_ref[...]
        for b in range(KV_GROUP):
            p = jnp.exp2(src_ref[rows(b), :] - m_new)
            lsum = lsum + jnp.sum(p, axis=0, keepdims=True)
            pv = pv + jnp.dot(vt_ref[gi * KV_GROUP + b], p.astype(BF16), preferred_element_type=F32)
        l_ref[...] = lsum
        acc_ref[...] = pv
        m_ref[...] = m_new

    n_full = (step * (tq // MOBA_BLOCK)) // KV_GROUP
    score(0, sa_ref)

    def pair(i, carry):
        score(2 * i + 1, sb_ref)
        absorb(sa_ref, 2 * i, False)
        score(2 * i + 2, sa_ref)
        absorb(sb_ref, 2 * i + 1, False)
        return carry

    lax.fori_loop(0, n_full // 2, pair, 0)

    @pl.when(n_full % 2 == 1)
    def _():
        score(n_full, sb_ref)
        absorb(sa_ref, n_full - 1, False)
        absorb(sb_ref, n_full, True)

    @pl.when(n_full % 2 == 0)
    def _():
        absorb(sa_ref, n_full, True)

    o_ref[...] = (acc_ref[...] / l_ref[...]).T


def _attn_post_kernel(o_ref, z_ref, h_ref, wo_ref, out_ref):
    gated = o_ref[...] * jax.nn.silu(z_ref[...])
    out_ref[...] = h_ref[...] + jnp.dot(gated.astype(BF16), wo_ref[...], preferred_element_type=F32)


def _rope_tables(seq):
    half = ROT_DIM // 2
    inv_freq = ROPE_THETA ** (-(np.arange(half, dtype=np.float64) * 2.0) / ROT_DIM)
    ang = np.arange(seq, dtype=np.float64)[:, None] * inv_freq[None, :]
    cos, sin = np.cos(ang), np.sin(ang)
    pad = HEAD_DIM - ROT_DIM
    zeros = np.zeros((seq, half))
    cosf = np.concatenate([cos, cos, np.ones((seq, pad))], axis=1)
    sa = np.concatenate([-sin, zeros, np.zeros((seq, pad))], axis=1)
    sb = np.concatenate([zeros, sin, np.zeros((seq, pad))], axis=1)
    return tuple(jnp.asarray(t, F32) for t in (cosf, sa, sb))


def _moba_layer(h, g, w_in, q_gain, k_gain, w_out, rope):
    seq, d = h.shape
    n_head = d // HEAD_DIM
    n_blk = seq // MOBA_BLOCK
    assert n_blk <= LANES and n_blk % KV_GROUP == 0 and seq % MOBA_BLOCK == 0
    cosf, sa, sb = rope
    w_qkz = jnp.concatenate([w_in[:, :2 * d], w_in[:, 3 * d:]], axis=1).astype(BF16)
    w_vt = w_in[:, 2 * d:3 * d].astype(BF16).T

    tm = 512
    bpt = tm // MOBA_BLOCK
    row = lambda i: (i, 0)
    const = lambda i: (0, 0)
    q, kaug, kmean, vt, z = pl.pallas_call(
        _attn_in_kernel,
        grid=(seq // tm,),
        in_specs=[pl.BlockSpec((tm, d), row),
                  pl.BlockSpec((1, d), const),
                  pl.BlockSpec((d, 3 * d), const),
                  pl.BlockSpec((d, d), const),
                  pl.BlockSpec((1, HEAD_DIM), const),
                  pl.BlockSpec((1, HEAD_DIM), const),
                  pl.BlockSpec((tm, HEAD_DIM), row),
                  pl.BlockSpec((tm, HEAD_DIM), row),
                  pl.BlockSpec((tm, HEAD_DIM), row)],
        out_specs=[pl.BlockSpec((n_head, tm, HEAD_DIM), lambda i: (0, i, 0)),
                   pl.BlockSpec((n_head, tm, 2 * HEAD_DIM), lambda i: (0, i, 0)),
                   pl.BlockSpec((bpt, n_head, HEAD_DIM), lambda i: (i, 0, 0)),
                   pl.BlockSpec((bpt, d, MOBA_BLOCK), lambda i: (i, 0, 0)),
                   pl.BlockSpec((tm, d), row)],
        out_shape=[jax.ShapeDtypeStruct((n_head, seq, HEAD_DIM), F32),
                   jax.ShapeDtypeStruct((n_head, seq, 2 * HEAD_DIM), BF16),
                   jax.ShapeDtypeStruct((n_blk, n_head, HEAD_DIM), F32),
                   jax.ShapeDtypeStruct((n_blk, d, MOBA_BLOCK), BF16),
                   jax.ShapeDtypeStruct((seq, d), F32)],
        compiler_params=_cparams("parallel"),
        name="attn_in",
    )(h, g.reshape(1, d), w_qkz, w_vt, q_gain.reshape(1, HEAD_DIM), k_gain.reshape(1, HEAD_DIM),
      cosf, sa, sb)

    qa = pl.pallas_call(
        _moba_select_kernel,
        grid=(n_head, seq // SEL_TILE),
        in_specs=[pl.BlockSpec((None, SEL_TILE, HEAD_DIM), lambda hd, i: (hd, i, 0)),
                  pl.BlockSpec((None, n_blk, HEAD_DIM), lambda hd, i: (hd, 0, 0))],
        out_specs=pl.BlockSpec((None, 2 * HEAD_DIM, SEL_TILE), lambda hd, i: (hd, 0, i)),
        out_shape=jax.ShapeDtypeStruct((n_head, 2 * HEAD_DIM, seq), BF16),
        compiler_params=_cparams("parallel", "parallel"),
        name="moba_select",
    )(q, kmean.transpose(1, 0, 2))

    o = pl.pallas_call(
        _moba_kernel,
        grid=(n_head, seq // Q_TILE),
        in_specs=[pl.BlockSpec((None, 2 * HEAD_DIM, Q_TILE), lambda hd, i: (hd, 0, i)),
                  pl.BlockSpec((None, seq, 2 * HEAD_DIM), lambda hd, i: (hd, 0, 0)),
                  pl.BlockSpec((n_blk, HEAD_DIM, MOBA_BLOCK), lambda hd, i: (0, hd, 0))],
        out_specs=pl.BlockSpec((Q_TILE, HEAD_DIM), lambda hd, i: (i, hd)),
        out_shape=jax.ShapeDtypeStruct((seq, d), F32),
        scratch_shapes=[pltpu.VMEM((KV_GROUP * MOBA_BLOCK, Q_TILE), F32),
                        pltpu.VMEM((KV_GROUP * MOBA_BLOCK, Q_TILE), F32),
                        pltpu.VMEM((1, Q_TILE), F32),
                        pltpu.VMEM((1, Q_TILE), F32),
                        pltpu.VMEM((HEAD_DIM, Q_TILE), F32)],
        compiler_params=_cparams("parallel", "arbitrary"),
        name="moba_attn",
    )(qa, kaug, vt)

    return pl.pallas_call(
        _attn_post_kernel,
        grid=(seq // tm,),
        in_specs=[pl.BlockSpec((tm, d), row),
                  pl.BlockSpec((tm, d), row),
                  pl.BlockSpec((tm, d), row),
                  pl.BlockSpec((d, d), const)],
        out_specs=pl.BlockSpec((tm, d), row),
        out_shape=jax.ShapeDtypeStruct((seq, d), F32),
        compiler_params=_cparams("parallel"),
        name="attn_post",
    )(o, z, h, w_out.astype(BF16))


def kernel(x, norm_g, ssm_w_in, ssm_a_re, ssm_a_im, ssm_log_dt, ssm_b_re, ssm_b_im, ssm_c_re, ssm_c_im, ssm_d, ssm_w_glu, ssm_b_glu, ssm_w_out, attn_w_in, attn_q_gain, attn_k_gain, attn_w_out):
    bsz, seq, d = x.shape
    depth = norm_g.shape[0]
    rope = _rope_tables(seq)
    outs = []
    for b in range(bsz):
        h = x[b]
        for i in range(depth):
            j = i // 2
            if i % 2 == 0:
                h = _s5_layer(h, norm_g[i], ssm_w_in[j], ssm_a_re[j], ssm_a_im[j], ssm_log_dt[j],
                              ssm_b_re[j], ssm_b_im[j], ssm_c_re[j], ssm_c_im[j], ssm_d[j],
                              ssm_w_glu[j], ssm_b_glu[j], ssm_w_out[j])
            else:
                h = _moba_layer(h, norm_g[i], attn_w_in[j], attn_q_gain[j], attn_k_gain[j],
                                attn_w_out[j], rope)
        outs.append(h)
    return jnp.stack(outs)
```

```python
import jax
import jax.numpy as jnp
import numpy as np
from jax import lax
from jax.experimental import pallas as pl
from jax.experimental.pallas import tpu as pltpu

F32 = jnp.float32
BF16 = jnp.bfloat16

NORM_EPS = 1e-6
SSM_GROUP = 16
SSM_STATE = 64
HEAD_DIM = 128
ROT_DIM = HEAD_DIM // 4
ROPE_THETA = 500000.0
MOBA_BLOCK = 256
MOBA_TOPK = 3

LANES = 128
SUBLANES = 8
SLAB_GROUPS = LANES // SSM_GROUP
SLAB_STATE = 2 * SLAB_GROUPS * SSM_STATE
SSM_CHUNK = 16
KV_GROUP = 4
NEG_BIG = -(2.0 ** 60)
LOG2_E = 1.4426950408889634
Q_TILE = 512
SEL_TILE = 2048
ROW_TILE = 512
CHUNK_TILE = 512
SCAN_TILE = 256
VMEM_LIMIT = 56 * 1024 * 1024

_NT = (((1,), (1,)), ((), ()))


def _cparams(*sem):
    return pltpu.CompilerParams(dimension_semantics=sem, vmem_limit_bytes=VMEM_LIMIT)


def _rms(x, g):
    ms = jnp.mean(x * x, axis=-1, keepdims=True)
    return x * lax.rsqrt(ms + NORM_EPS) * g


def _log2(n):
    assert n & (n - 1) == 0
    return n.bit_length() - 1


def _s5_in_kernel(h_ref, g_ref, w_ref, u_ref, z_ref, us_ref):
    tm, d = h_ref.shape
    n_row = tm // SSM_CHUNK
    hn = _rms(h_ref[...], g_ref[...]).astype(BF16)
    proj = jnp.dot(hn, w_ref[...], preferred_element_type=F32)
    z_ref[...] = proj[:, d:].astype(BF16)
    for s in range(d // LANES):
        us_ref[s] = proj[:, s * LANES:(s + 1) * LANES]
        for t in range(SSM_CHUNK):
            u_ref[s, :, t * LANES:(t + 1) * LANES] = us_ref[s, pl.ds(t, n_row, stride=SSM_CHUNK), :]


def _expand_rows(re2, im2):
    r = lax.broadcasted_iota(jnp.int32, re2.shape, 0)
    lane = lax.broadcasted_iota(jnp.int32, re2.shape, 1)
    row_group = r >> _log2(SSM_GROUP)
    lane_half = lane >> _log2(SSM_STATE)
    cols_re, cols_im = [], []
    for j in range(SLAB_STATE // 2 // LANES):
        msk = row_group == 2 * j + lane_half
        cols_re.append(jnp.where(msk, re2, 0.0))
        cols_im.append(jnp.where(msk, im2, 0.0))
    return jnp.concatenate(cols_re + cols_im, axis=1)


def _ssm_state_kernel(u_ref, re2_ref, im2_ref, z_ref, wz_ref):
    @pl.when(pl.program_id(1) == 0)
    def _():
        for k in range(re2_ref.shape[0]):
            wz_ref[k * LANES:(k + 1) * LANES, :] = _expand_rows(re2_ref[k], im2_ref[k]).astype(BF16)

    z_ref[...] = jnp.dot(u_ref[...].astype(BF16), wz_ref[...], preferred_element_type=F32)


def _ssm_scan_kernel(z_ref, ap_ref, o_ref, last_ref):
    half = SLAB_STATE // 2
    n_slab = z_ref.shape[1] // SLAB_STATE

    @pl.when(pl.program_id(0) == 0)
    def _():
        last_ref[...] = jnp.zeros_like(last_ref)

    rows = lax.broadcasted_iota(jnp.int32, (SUBLANES, half), 0)

    def shift_down(x, k, fill):
        return jnp.where(rows >= k, pltpu.roll(x, k, 0), fill)

    def body(step, _):
        r0 = pl.multiple_of(step * SUBLANES, SUBLANES)
        for s in range(n_slab):
            re_sl = slice(s * SLAB_STATE, s * SLAB_STATE + half)
            im_sl = slice(s * SLAB_STATE + half, (s + 1) * SLAB_STATE)
            xr = z_ref[pl.ds(r0, SUBLANES), re_sl]
            xi = z_ref[pl.ds(r0, SUBLANES), im_sl]
            for k in (1, 2, 4):
                ar = ap_ref[k - 1:k, re_sl]
                ai = ap_ref[k - 1:k, im_sl]
                sr, si = shift_down(xr, k, 0.0), shift_down(xi, k, 0.0)
                xr, xi = xr + ar * sr - ai * si, xi + ar * si + ai * sr
            cr = jnp.broadcast_to(last_ref[SUBLANES - 1:SUBLANES, re_sl], (SUBLANES, half))
            ci = jnp.broadcast_to(last_ref[SUBLANES - 1:SUBLANES, im_sl], (SUBLANES, half))
            apr, api = ap_ref[:, re_sl], ap_ref[:, im_sl]
            xr, xi = xr + apr * cr - api * ci, xi + apr * ci + api * cr
            o_ref[pl.ds(r0, SUBLANES), re_sl] = shift_down(xr, 1, cr)
            o_ref[pl.ds(r0, SUBLANES), im_sl] = shift_down(xi, 1, ci)
            last_ref[:, re_sl] = xr
            last_ref[:, im_sl] = xi
        return 0

    lax.fori_loop(0, z_ref.shape[0] // SUBLANES, body, 0)


def _ssm_out_kernel(u_ref, sp_ref, cre_ref, cim_ref, fre_ref, fim_ref, d_ref, y_ref,
                    m_ref, ct_ref, clo_ref):
    t_len = SSM_CHUNK
    tr = u_ref.shape[0]
    tw = t_len * LANES

    @pl.when(pl.program_id(1) == 0)
    def _():
        for t in range(t_len + 1):
            x = _expand_rows(cre_ref[t], cim_ref[t])
            hi = x.astype(BF16)
            ct_ref[t * LANES:(t + 1) * LANES, :] = hi
            if t < t_len:
                clo_ref[t * LANES:(t + 1) * LANES, :] = (x - hi.astype(F32)).astype(BF16)
        fb = _expand_rows(fre_ref[...], fim_ref[...])
        fb_hi = fb.astype(BF16)
        fb_lo = (fb - fb_hi.astype(F32)).astype(BF16)
        c_hi, c_lo = ct_ref[:tw, :], clo_ref[...]
        krow = (lax.dot_general(fb_hi, c_hi, _NT, preferred_element_type=F32)
                + lax.dot_general(fb_lo, c_hi, _NT, preferred_element_type=F32)
                + lax.dot_general(fb_hi, c_lo, _NT, preferred_element_type=F32)).astype(BF16)
        for k in range(t_len):
            parts = [krow[:, :(t_len - k) * LANES]]
            if k:
                parts = [jnp.zeros((LANES, k * LANES), BF16)] + parts
            m_ref[k * LANES:(k + 1) * LANES, :] = jnp.concatenate(parts, axis=1)

    u = u_ref[...]
    ub = u.astype(BF16)
    sp = sp_ref[...].astype(BF16)
    for j in range(t_len // 2):
        lo, hi = 2 * j * LANES, (2 * j + 2) * LANES
        y = jnp.dot(ub[:, :hi], m_ref[:hi, lo:hi], preferred_element_type=F32)
        y = y + lax.dot_general(sp, ct_ref[LANES + lo:LANES + hi, :], _NT, preferred_element_type=F32)
        y = jax.nn.gelu(y + d_ref[:, lo:hi] * u[:, lo:hi], approximate=True)
        for t in (2 * j, 2 * j + 1):
            y_ref[pl.ds(t, tr, stride=t_len), :] = y[:, t * LANES - lo:(t + 1) * LANES - lo]


def _s5_post_kernel(y_ref, z_ref, h_ref, wg_ref, bg_ref, wo_ref, o_ref):
    y = jnp.concatenate([y_ref[s] for s in range(y_ref.shape[0])], axis=-1)
    lin = jnp.dot(y.astype(BF16), wg_ref[...], preferred_element_type=F32) + bg_ref[...]
    gated = y * jax.nn.sigmoid(lin) * jax.nn.silu(z_ref[...].astype(F32))
    o_ref[...] = h_ref[...] + jnp.dot(gated.astype(BF16), wo_ref[...], preferred_element_type=F32)


def _s5_derived(a_re, a_im, log_dt, b_re, b_im, c_re, c_im, d_skip):
    t_len = SSM_CHUNK
    g_cnt, p_cnt = a_re.shape
    n_slab = g_cnt // SLAB_GROUPS
    dt = jnp.exp(log_dt)[:, None]

    def dup(x):
        return jnp.concatenate([x, x], axis=-1)

    def powers(exps, lam_re, lam_im):
        j = exps.astype(F32)[:, None, None]
        mag_j = jnp.exp(j * lam_re)
        return mag_j * jnp.cos(j * lam_im), mag_j * jnp.sin(j * lam_im)

    lr, li = dup(a_re), dup(a_im)
    lam_re, lam_im = lr * dt, li * dt
    mag = jnp.exp(lam_re)
    ab_re, ab_im = mag * jnp.cos(lam_im), mag * jnp.sin(lam_im)
    den = lr * lr + li * li
    nr, ni = ab_re - 1.0, ab_im
    f_re = (nr * lr + ni * li) / den
    f_im = (ni * lr - nr * li) / den
    pr, pi = powers(t_len - 1 - jnp.arange(t_len), lam_re, lam_im)
    w_re = (pr * f_re - pi * f_im)[:, :, None, :]
    w_im = (pr * f_im + pi * f_re)[:, :, None, :]
    bt_re, bt_im = dup(b_re.transpose(0, 2, 1)), dup(b_im.transpose(0, 2, 1))
    wz_re = w_re * bt_re - w_im * bt_im
    wz_im = w_re * bt_im + w_im * bt_re
    qr, qi = powers(jnp.arange(t_len + 1), lam_re, lam_im)
    qr, qi = qr[:, :, None, :], qi[:, :, None, :]
    ct_re, ct_im = dup(c_re), dup(c_im)
    cv_re = ct_re * qr - ct_im * qi
    cv_im = -(ct_re * qi + ct_im * qr)

    def rows(x):
        return x.reshape(x.shape[0], g_cnt * SSM_GROUP, 2 * p_cnt)

    ap_re, ap_im = powers(t_len * jnp.arange(1, SUBLANES + 1), a_re * dt, a_im * dt)
    a_pow = jnp.stack([ap_re.reshape(SUBLANES, n_slab, -1), ap_im.reshape(SUBLANES, n_slab, -1)], axis=2)
    a_pow = a_pow.reshape(SUBLANES, n_slab * SLAB_STATE)
    d_til = jnp.tile(d_skip.reshape(n_slab, 1, LANES), (1, 1, t_len))
    return rows(wz_re), rows(wz_im), rows(cv_re), rows(cv_im), a_pow, d_til


def _s5_layer(h, g, w_in, a_re, a_im, log_dt, b_re, b_im, c_re, c_im, d_skip, w_glu, b_glu, w_out):
    seq, d = h.shape
    n_slab = d // LANES
    t_len = SSM_CHUNK
    n_chunk = seq // t_len
    tw = t_len * LANES
    sw = SLAB_STATE
    wz_re, wz_im, cv_re, cv_im, a_pow, d_til = _s5_derived(
        a_re, a_im, log_dt, b_re, b_im, c_re, c_im, d_skip)

    tm = ROW_TILE
    uc, z = pl.pallas_call(
        _s5_in_kernel,
        grid=(seq // tm,),
        in_specs=[pl.BlockSpec((tm, d), lambda i: (i, 0)),
                  pl.BlockSpec((1, d), lambda i: (0, 0)),
                  pl.BlockSpec((d, 2 * d), lambda i: (0, 0))],
        out_specs=[pl.BlockSpec((n_slab, tm // t_len, tw), lambda i: (0, i, 0)),
                   pl.BlockSpec((tm, d), lambda i: (i, 0))],
        out_shape=[jax.ShapeDtypeStruct((n_slab, n_chunk, tw), F32),
                   jax.ShapeDtypeStruct((seq, d), BF16)],
        scratch_shapes=[pltpu.VMEM((n_slab, tm, LANES), F32)],
        compiler_params=_cparams("parallel"),
        name="s5_in",
    )(h, g.reshape(1, d), w_in.astype(BF16))

    tr = CHUNK_TILE
    steps = lambda s, i: (0, s, 0)
    zst = pl.pallas_call(
        _ssm_state_kernel,
        grid=(n_slab, n_chunk // tr),
        in_specs=[pl.BlockSpec((None, tr, tw), lambda s, i: (s, i, 0)),
                  pl.BlockSpec((t_len, LANES, LANES), steps),
                  pl.BlockSpec((t_len, LANES, LANES), steps)],
        out_specs=pl.BlockSpec((tr, sw), lambda s, i: (i, s)),
        out_shape=jax.ShapeDtypeStruct((n_chunk, n_slab * sw), F32),
        scratch_shapes=[pltpu.VMEM((tw, sw), BF16)],
        compiler_params=_cparams("parallel", "arbitrary"),
        name="ssm_state",
    )(uc, wz_re, wz_im)

    tc = SCAN_TILE
    sprev = pl.pallas_call(
        _ssm_scan_kernel,
        grid=(n_chunk // tc,),
        in_specs=[pl.BlockSpec((tc, n_slab * sw), lambda i: (i, 0)),
                  pl.BlockSpec((SUBLANES, n_slab * sw), lambda i: (0, 0))],
        out_specs=pl.BlockSpec((tc, n_slab * sw), lambda i: (i, 0)),
        out_shape=jax.ShapeDtypeStruct((n_chunk, n_slab * sw), F32),
        scratch_shapes=[pltpu.VMEM((SUBLANES, n_slab * sw), F32)],
        compiler_params=_cparams("arbitrary"),
        name="ssm_scan",
    )(zst, a_pow)

    last_step = lambda s, i: (t_len - 1, s, 0)
    yg = pl.pallas_call(
        _ssm_out_kernel,
        grid=(n_slab, n_chunk // tr),
        in_specs=[pl.BlockSpec((None, tr, tw), lambda s, i: (s, i, 0)),
                  pl.BlockSpec((tr, sw), lambda s, i: (i, s)),
                  pl.BlockSpec((t_len + 1, LANES, LANES), steps),
                  pl.BlockSpec((t_len + 1, LANES, LANES), steps),
                  pl.BlockSpec((None, LANES, LANES), last_step),
                  pl.BlockSpec((None, LANES, LANES), last_step),
                  pl.BlockSpec((None, 1, tw), lambda s, i: (s, 0, 0))],
        out_specs=pl.BlockSpec((None, tr * t_len, LANES), lambda s, i: (s, i, 0)),
        out_shape=jax.ShapeDtypeStruct((n_slab, seq, LANES), F32),
        scratch_shapes=[pltpu.VMEM((tw, tw), BF16),
                        pltpu.VMEM((tw + LANES, sw), BF16),
                        pltpu.VMEM((tw, sw), BF16)],
        compiler_params=_cparams("parallel", "arbitrary"),
        name="ssm_out",
    )(uc, sprev, cv_re, cv_im, wz_re, wz_im, d_til)

    return pl.pallas_call(
        _s5_post_kernel,
        grid=(seq // tm,),
        in_specs=[pl.BlockSpec((n_slab, tm, LANES), lambda i: (0, i, 0)),
                  pl.BlockSpec((tm, d), lambda i: (i, 0)),
                  pl.BlockSpec((tm, d), lambda i: (i, 0)),
                  pl.BlockSpec((d, d), lambda i: (0, 0)),
                  pl.BlockSpec((1, d), lambda i: (0, 0)),
                  pl.BlockSpec((d, d), lambda i: (0, 0))],
        out_specs=pl.BlockSpec((tm, d), lambda i: (i, 0)),
        out_shape=jax.ShapeDtypeStruct((seq, d), F32),
        compiler_params=_cparams("parallel"),
        name="s5_post",
    )(yg, z, h, w_glu.astype(BF16), b_glu.reshape(1, d), w_out.astype(BF16))


def _attn_in_kernel(h_ref, g_ref, wqkz_ref, wvt_ref, qg_ref, kg_ref, cos_ref, sa_ref, sb_ref,
                    q_ref, ka_ref, km_ref, vt_ref, z_ref):
    tm, d = h_ref.shape
    n_head = d // HEAD_DIM
    n_blk = tm // MOBA_BLOCK
    i = pl.program_id(0)
    hn = _rms(h_ref[...], g_ref[...]).astype(BF16)
    qkz = jnp.dot(hn, wqkz_ref[...], preferred_element_type=F32)
    vt = lax.dot_general(wvt_ref[...], hn, _NT, preferred_element_type=F32)
    cosf, sa, sb = cos_ref[...], sa_ref[...], sb_ref[...]

    def norm_rope(t, gain):
        t = _rms(t, gain)
        up = pltpu.roll(t, HEAD_DIM - ROT_DIM // 2, 1)
        dn = pltpu.roll(t, ROT_DIM // 2, 1)
        return t * cosf + up * sa + dn * sb

    scale = HEAD_DIM ** -0.5 * LOG2_E
    for hd in range(n_head):
        qh = norm_rope(qkz[:, hd * HEAD_DIM:(hd + 1) * HEAD_DIM], qg_ref[...])
        q_ref[hd] = qh * scale
        kh = norm_rope(qkz[:, d + hd * HEAD_DIM:d + (hd + 1) * HEAD_DIM], kg_ref[...])
        ka_ref[hd, :, :HEAD_DIM] = kh.astype(BF16)
        for b in range(n_blk):
            km_ref[b, hd:hd + 1, :] = jnp.mean(
                kh[b * MOBA_BLOCK:(b + 1) * MOBA_BLOCK], axis=0, keepdims=True)
    lane = lax.broadcasted_iota(jnp.int32, (MOBA_BLOCK, LANES), 1)
    for b in range(n_blk):
        onehot = jnp.where(lane == i * n_blk + b, NEG_BIG, 0.0).astype(BF16)
        for hd in range(n_head):
            ka_ref[hd, b * MOBA_BLOCK:(b + 1) * MOBA_BLOCK, HEAD_DIM:] = onehot
        vt_ref[b] = vt[:, b * MOBA_BLOCK:(b + 1) * MOBA_BLOCK].astype(BF16)
    z_ref[...] = qkz[:, 2 * d:].astype(BF16)


def _moba_select_kernel(q_ref, km_ref, qa_ref):
    ts, hd = q_ref.shape
    n_blk = km_ref.shape[0]
    q = q_ref[...]
    gates = lax.dot_general(km_ref[...], q, _NT, precision=lax.Precision.HIGHEST,
                            preferred_element_type=F32)
    nidx = lax.broadcasted_iota(jnp.int32, gates.shape, 0)
    qpos = pl.program_id(1) * ts + lax.broadcasted_iota(jnp.int32, gates.shape, 1)
    own = qpos >> _log2(MOBA_BLOCK)
    valid = nidx < own
    g = jnp.where(valid, gates, -jnp.inf)
    picked = jnp.zeros(gates.shape, jnp.bool_)
    for _ in range(MOBA_TOPK):
        mx = jnp.max(g, axis=0, keepdims=True)
        first = jnp.min(jnp.where(g == mx, nidx, n_blk), axis=0, keepdims=True)
        hit = nidx == first
        picked = jnp.logical_or(picked, hit)
        g = jnp.where(hit, -jnp.inf, g)
    visible = jnp.logical_or(jnp.logical_and(valid, picked), nidx == own)
    qa_ref[:hd, :] = q.T.astype(BF16)
    qa_ref[hd:hd + n_blk, :] = jnp.where(visible, 0.0, 1.0).astype(BF16)
    qa_ref[hd + n_blk:, :] = jnp.zeros((qa_ref.shape[0] - hd - n_blk, ts), BF16)


def _moba_kernel(qa_ref, k_ref, vt_ref, o_ref, sa_ref, sb_ref, m_ref, l_ref, acc_ref):
    step = pl.program_id(1)
    tq = qa_ref.shape[1]
    gk = KV_GROUP * MOBA_BLOCK
    assert KV_GROUP % (tq // MOBA_BLOCK) == 0

    m_ref[...] = jnp.full(m_ref.shape, -jnp.inf, F32)
    l_ref[...] = jnp.zeros_like(l_ref)
    acc_ref[...] = jnp.zeros_like(acc_ref)

    def score(gi, dst_ref):
        kt = k_ref[pl.ds(pl.multiple_of(gi * gk, gk), gk), :]
        dst_ref[...] = jnp.dot(kt, qa_ref[...], preferred_element_type=F32)

    def rows(b):
        return slice(b * MOBA_BLOCK, (b + 1) * MOBA_BLOCK)

    def absorb(src_ref, gi, causal):
        if causal:
            kpos = gi * gk + lax.broadcasted_iota(jnp.int32, src_ref.shape, 0)
            qpos = step * tq + lax.broadcasted_iota(jnp.int32, src_ref.shape, 1)
            src_ref[...] = jnp.where(kpos <= qpos, src_ref[...], -jnp.inf)
        m_old = m_ref[...]
        m_new = m_old
        for b in range(KV_GROUP):
            m_new = jnp.maximum(m_new, jnp.max(src_ref[rows(b), :], axis=0, keepdims=True))
        alpha = jnp.exp2(m_old - m_new)
        lsum = alpha * l_ref[...]
        pv = alpha * acc_ref[...]
        for b in range(KV_GROUP):
            p = jnp.exp2(src_ref[rows(b), :] - m_new)
            lsum = lsum + jnp.sum(p, axis=0, keepdims=True)
            pv = pv + jnp.dot(vt_ref[gi * KV_GROUP + b], p.astype(BF16), preferred_element_type=F32)
        l_ref[...] = lsum
        acc_ref[...] = pv
        m_ref[...] = m_new

    n_full = (step * (tq // MOBA_BLOCK)) // KV_GROUP
    score(0, sa_ref)

    def pair(i, carry):
        score(2 * i + 1, sb_ref)
        absorb(sa_ref, 2 * i, False)
        score(2 * i + 2, sa_ref)
        absorb(sb_ref, 2 * i + 1, False)
        return carry

    lax.fori_loop(0, n_full // 2, pair, 0)

    @pl.when(n_full % 2 == 1)
    def _():
        score(n_full, sb_ref)
        absorb(sa_ref, n_full - 1, False)
        absorb(sb_ref, n_full, True)

    @pl.when(n_full % 2 == 0)
    def _():
        absorb(sa_ref, n_full, True)

    o_ref[...] = (acc_ref[...] / l_ref[...]).T.astype(o_ref.dtype)


def _attn_post_kernel(o_ref, z_ref, h_ref, wo_ref, out_ref):
    gated = o_ref[...].astype(F32) * jax.nn.silu(z_ref[...].astype(F32))
    out_ref[...] = h_ref[...] + jnp.dot(gated.astype(BF16), wo_ref[...], preferred_element_type=F32)


def _rope_tables(seq):
    half = ROT_DIM // 2
    inv_freq = ROPE_THETA ** (-(np.arange(half, dtype=np.float64) * 2.0) / ROT_DIM)
    ang = np.arange(seq, dtype=np.float64)[:, None] * inv_freq[None, :]
    cos, sin = np.cos(ang), np.sin(ang)
    pad = HEAD_DIM - ROT_DIM
    zeros = np.zeros((seq, half))
    cosf = np.concatenate([cos, cos, np.ones((seq, pad))], axis=1)
    sa = np.concatenate([-sin, zeros, np.zeros((seq, pad))], axis=1)
    sb = np.concatenate([zeros, sin, np.zeros((seq, pad))], axis=1)
    return tuple(jnp.asarray(t, F32) for t in (cosf, sa, sb))


def _moba_layer(h, g, w_in, q_gain, k_gain, w_out, rope):
    seq, d = h.shape
    n_head = d // HEAD_DIM
    n_blk = seq // MOBA_BLOCK
    assert n_blk <= LANES and n_blk % KV_GROUP == 0 and seq % MOBA_BLOCK == 0
    cosf, sa, sb = rope
    w_qkz = jnp.concatenate([w_in[:, :2 * d], w_in[:, 3 * d:]], axis=1).astype(BF16)
    w_vt = w_in[:, 2 * d:3 * d].astype(BF16).T

    tm = ROW_TILE
    bpt = tm // MOBA_BLOCK
    row = lambda i: (i, 0)
    const = lambda i: (0, 0)
    q, kaug, kmean, vt, z = pl.pallas_call(
        _attn_in_kernel,
        grid=(seq // tm,),
        in_specs=[pl.BlockSpec((tm, d), row),
                  pl.BlockSpec((1, d), const),
                  pl.BlockSpec((d, 3 * d), const),
                  pl.BlockSpec((d, d), const),
                  pl.BlockSpec((1, HEAD_DIM), const),
                  pl.BlockSpec((1, HEAD_DIM), const),
                  pl.BlockSpec((tm, HEAD_DIM), row),
                  pl.BlockSpec((tm, HEAD_DIM), row),
                  pl.BlockSpec((tm, HEAD_DIM), row)],
        out_specs=[pl.BlockSpec((n_head, tm, HEAD_DIM), lambda i: (0, i, 0)),
                   pl.BlockSpec((n_head, tm, 2 * HEAD_DIM), lambda i: (0, i, 0)),
                   pl.BlockSpec((bpt, n_head, HEAD_DIM), lambda i: (i, 0, 0)),
                   pl.BlockSpec((bpt, d, MOBA_BLOCK), lambda i: (i, 0, 0)),
                   pl.BlockSpec((tm, d), row)],
        out_shape=[jax.ShapeDtypeStruct((n_head, seq, HEAD_DIM), F32),
                   jax.ShapeDtypeStruct((n_head, seq, 2 * HEAD_DIM), BF16),
                   jax.ShapeDtypeStruct((n_blk, n_head, HEAD_DIM), F32),
                   jax.ShapeDtypeStruct((n_blk, d, MOBA_BLOCK), BF16),
                   jax.ShapeDtypeStruct((seq, d), BF16)],
        compiler_params=_cparams("parallel"),
        name="attn_in",
    )(h, g.reshape(1, d), w_qkz, w_vt, q_gain.reshape(1, HEAD_DIM), k_gain.reshape(1, HEAD_DIM),
      cosf, sa, sb)

    qa = pl.pallas_call(
        _moba_select_kernel,
        grid=(n_head, seq // SEL_TILE),
        in_specs=[pl.BlockSpec((None, SEL_TILE, HEAD_DIM), lambda hd, i: (hd, i, 0)),
                  pl.BlockSpec((None, n_blk, HEAD_DIM), lambda hd, i: (hd, 0, 0))],
        out_specs=pl.BlockSpec((None, 2 * HEAD_DIM, SEL_TILE), lambda hd, i: (hd, 0, i)),
        out_shape=jax.ShapeDtypeStruct((n_head, 2 * HEAD_DIM, seq), BF16),
        compiler_params=_cparams("parallel", "parallel"),
        name="moba_select",
    )(q, kmean.transpose(1, 0, 2))

    o = pl.pallas_call(
        _moba_kernel,
        grid=(n_head, seq // Q_TILE),
        in_specs=[pl.BlockSpec((None, 2 * HEAD_DIM, Q_TILE), lambda hd, i: (hd, 0, i)),
                  pl.BlockSpec((None, seq, 2 * HEAD_DIM), lambda hd, i: (hd, 0, 0)),
                  pl.BlockSpec((n_blk, HEAD_DIM, MOBA_BLOCK), lambda hd, i: (0, hd, 0))],
        out_specs=pl.BlockSpec((Q_TILE, HEAD_DIM), lambda hd, i: (i, hd)),
        out_shape=jax.ShapeDtypeStruct((seq, d), BF16),
        scratch_shapes=[pltpu.VMEM((KV_GROUP * MOBA_BLOCK, Q_TILE), F32),
                        pltpu.VMEM((KV_GROUP * MOBA_BLOCK, Q_TILE), F32),
                        pltpu.VMEM((1, Q_TILE), F32),
                        pltpu.VMEM((1, Q_TILE), F32),
                        pltpu.VMEM((HEAD_DIM, Q_TILE), F32)],
        compiler_params=_cparams("parallel", "arbitrary"),
        name="moba_attn",
    )(qa, kaug, vt)

    return pl.pallas_call(
        _attn_post_kernel,
        grid=(seq // tm,),
        in_specs=[pl.BlockSpec((tm, d), row),
                  pl.BlockSpec((tm, d), row),
                  pl.BlockSpec((tm, d), row),
                  pl.BlockSpec((d, d), const)],
        out_specs=pl.BlockSpec((tm, d), row),
        out_shape=jax.ShapeDtypeStruct((seq, d), F32),
        compiler_params=_cparams("parallel"),
        name="attn_post",
    )(o, z, h, w_out.astype(BF16))


def kernel(x, norm_g, ssm_w_in, ssm_a_re, ssm_a_im, ssm_log_dt, ssm_b_re, ssm_b_im, ssm_c_re, ssm_c_im, ssm_d, ssm_w_glu, ssm_b_glu, ssm_w_out, attn_w_in, attn_q_gain, attn_k_gain, attn_w_out):
    bsz, seq, d = x.shape
    depth = norm_g.shape[0]
    rope = _rope_tables(seq)
    outs = []
    for b in range(bsz):
        h = x[b]
        for i in range(depth):
            j = i // 2
            if i % 2 == 0:
                h = _s5_layer(h, norm_g[i], ssm_w_in[j], ssm_a_re[j], ssm_a_im[j], ssm_log_dt[j],
                              ssm_b_re[j], ssm_b_im[j], ssm_c_re[j], ssm_c_im[j], ssm_d[j],
                              ssm_w_glu[j], ssm_b_glu[j], ssm_w_out[j])
            else:
                h = _moba_layer(h, norm_g[i], attn_w_in[j], attn_q_gain[j], attn_k_gain[j],
                                attn_w_out[j], rope)
        outs.append(h)
    return jnp.stack(outs)
```

```python
import jax
import jax.numpy as jnp
import numpy as np
from jax import lax
from jax.experimental import pallas as pl
from jax.experimental.pallas import tpu as pltpu

F32 = jnp.float32
BF16 = jnp.bfloat16

NORM_EPS = 1e-6
SSM_GROUP = 16
SSM_STATE = 64
HEAD_DIM = 128
ROT_DIM = HEAD_DIM // 4
ROPE_THETA = 500000.0
MOBA_BLOCK = 256
MOBA_TOPK = 3

LANES = 128
SUBLANES = 8
SLAB_GROUPS = LANES // SSM_GROUP
SLAB_STATE = 2 * SLAB_GROUPS * SSM_STATE
SSM_CHUNK = 16
KV_GROUP = 4
NEG_BIG = -(2.0 ** 60)
LOG2_E = 1.4426950408889634
Q_TILE = 1024
SEL_TILE = 2048
ROW_TILE = 512
CHUNK_TILE = 512
SCAN_TILE = 256
VMEM_LIMIT = 56 * 1024 * 1024

_NT = (((1,), (1,)), ((), ()))


def _cparams(*sem):
    return pltpu.CompilerParams(dimension_semantics=sem, vmem_limit_bytes=VMEM_LIMIT)


def _rms(x, g):
    ms = jnp.mean(x * x, axis=-1, keepdims=True)
    return x * lax.rsqrt(ms + NORM_EPS) * g


def _log2(n):
    assert n & (n - 1) == 0
    return n.bit_length() - 1


def _s5_in_kernel(h_ref, g_ref, w_ref, u_ref, z_ref, us_ref):
    tm, d = h_ref.shape
    n_row = tm // SSM_CHUNK
    hn = _rms(h_ref[...], g_ref[...]).astype(BF16)
    proj = jnp.dot(hn, w_ref[...], preferred_element_type=F32)
    z_ref[...] = proj[:, d:].astype(BF16)
    for s in range(d // LANES):
        us_ref[s] = proj[:, s * LANES:(s + 1) * LANES]
        for t in range(SSM_CHUNK):
            u_ref[s, :, t * LANES:(t + 1) * LANES] = us_ref[s, pl.ds(t, n_row, stride=SSM_CHUNK), :]


def _expand_rows(re2, im2):
    r = lax.broadcasted_iota(jnp.int32, re2.shape, 0)
    lane = lax.broadcasted_iota(jnp.int32, re2.shape, 1)
    row_group = r >> _log2(SSM_GROUP)
    lane_half = lane >> _log2(SSM_STATE)
    cols_re, cols_im = [], []
    for j in range(SLAB_STATE // 2 // LANES):
        msk = row_group == 2 * j + lane_half
        cols_re.append(jnp.where(msk, re2, 0.0))
        cols_im.append(jnp.where(msk, im2, 0.0))
    return jnp.concatenate(cols_re + cols_im, axis=1)


def _ssm_state_kernel(u_ref, re2_ref, im2_ref, z_ref, wz_ref):
    @pl.when(pl.program_id(1) == 0)
    def _():
        for k in range(re2_ref.shape[0]):
            wz_ref[k * LANES:(k + 1) * LANES, :] = _expand_rows(re2_ref[k], im2_ref[k]).astype(BF16)

    z_ref[...] = jnp.dot(u_ref[...].astype(BF16), wz_ref[...], preferred_element_type=F32)


def _ssm_scan_kernel(z_ref, ap_ref, o_ref, last_ref):
    half = SLAB_STATE // 2
    n_slab = z_ref.shape[1] // SLAB_STATE

    @pl.when(pl.program_id(0) == 0)
    def _():
        last_ref[...] = jnp.zeros_like(last_ref)

    rows = lax.broadcasted_iota(jnp.int32, (SUBLANES, half), 0)

    def shift_down(x, k, fill):
        return jnp.where(rows >= k, pltpu.roll(x, k, 0), fill)

    def body(step, _):
        r0 = pl.multiple_of(step * SUBLANES, SUBLANES)
        for s in range(n_slab):
            re_sl = slice(s * SLAB_STATE, s * SLAB_STATE + half)
            im_sl = slice(s * SLAB_STATE + half, (s + 1) * SLAB_STATE)
            xr = z_ref[pl.ds(r0, SUBLANES), re_sl]
            xi = z_ref[pl.ds(r0, SUBLANES), im_sl]
            for k in (1, 2, 4):
                ar = ap_ref[k - 1:k, re_sl]
                ai = ap_ref[k - 1:k, im_sl]
                sr, si = shift_down(xr, k, 0.0), shift_down(xi, k, 0.0)
                xr, xi = xr + ar * sr - ai * si, xi + ar * si + ai * sr
            cr = jnp.broadcast_to(last_ref[SUBLANES - 1:SUBLANES, re_sl], (SUBLANES, half))
            ci = jnp.broadcast_to(last_ref[SUBLANES - 1:SUBLANES, im_sl], (SUBLANES, half))
            apr, api = ap_ref[:, re_sl], ap_ref[:, im_sl]
            xr, xi = xr + apr * cr - api * ci, xi + apr * ci + api * cr
            o_ref[pl.ds(r0, SUBLANES), re_sl] = shift_down(xr, 1, cr)
            o_ref[pl.ds(r0, SUBLANES), im_sl] = shift_down(xi, 1, ci)
            last_ref[:, re_sl] = xr
            last_ref[:, im_sl] = xi
        return 0

    lax.fori_loop(0, z_ref.shape[0] // SUBLANES, body, 0)


def _ssm_out_kernel(u_ref, sp_ref, cre_ref, cim_ref, fre_ref, fim_ref, d_ref, y_ref,
                    m_ref, ct_ref, clo_ref):
    t_len = SSM_CHUNK
    tr = u_ref.shape[0]
    tw = t_len * LANES

    @pl.when(pl.program_id(1) == 0)
    def _():
        for t in range(t_len + 1):
            x = _expand_rows(cre_ref[t], cim_ref[t])
            hi = x.astype(BF16)
            ct_ref[t * LANES:(t + 1) * LANES, :] = hi
            if t < t_len:
                clo_ref[t * LANES:(t + 1) * LANES, :] = (x - hi.astype(F32)).astype(BF16)
        fb = _expand_rows(fre_ref[...], fim_ref[...])
        fb_hi = fb.astype(BF16)
        fb_lo = (fb - fb_hi.astype(F32)).astype(BF16)
        c_hi, c_lo = ct_ref[:tw, :], clo_ref[...]
        krow = (lax.dot_general(fb_hi, c_hi, _NT, preferred_element_type=F32)
                + lax.dot_general(fb_lo, c_hi, _NT, preferred_element_type=F32)
                + lax.dot_general(fb_hi, c_lo, _NT, preferred_element_type=F32)).astype(BF16)
        for k in range(t_len):
            parts = [krow[:, :(t_len - k) * LANES]]
            if k:
                parts = [jnp.zeros((LANES, k * LANES), BF16)] + parts
            m_ref[k * LANES:(k + 1) * LANES, :] = jnp.concatenate(parts, axis=1)

    u = u_ref[...]
    ub = u.astype(BF16)
    sp = sp_ref[...].astype(BF16)
    for j in range(t_len // 2):
        lo, hi = 2 * j * LANES, (2 * j + 2) * LANES
        y = jnp.dot(ub[:, :hi], m_ref[:hi, lo:hi], preferred_element_type=F32)
        y = y + lax.dot_general(sp, ct_ref[LANES + lo:LANES + hi, :], _NT, preferred_element_type=F32)
        y = jax.nn.gelu(y + d_ref[:, lo:hi] * u[:, lo:hi], approximate=True)
        for t in (2 * j, 2 * j + 1):
            y_ref[pl.ds(t, tr, stride=t_len), :] = y[:, t * LANES - lo:(t + 1) * LANES - lo]


def _s5_post_kernel(y_ref, z_ref, h_ref, wg_ref, bg_ref, wo_ref, o_ref):
    y = jnp.concatenate([y_ref[s] for s in range(y_ref.shape[0])], axis=-1)
    lin = jnp.dot(y.astype(BF16), wg_ref[...], preferred_element_type=F32) + bg_ref[...]
    gated = y * jax.nn.sigmoid(lin) * jax.nn.silu(z_ref[...].astype(F32))
    o_ref[...] = h_ref[...] + jnp.dot(gated.astype(BF16), wo_ref[...], preferred_element_type=F32)


def _s5_derived(a_re, a_im, log_dt, b_re, b_im, c_re, c_im, d_skip):
    t_len = SSM_CHUNK
    g_cnt, p_cnt = a_re.shape
    n_slab = g_cnt // SLAB_GROUPS
    dt = jnp.exp(log_dt)[:, None]

    def dup(x):
        return jnp.concatenate([x, x], axis=-1)

    def powers(exps, lam_re, lam_im):
        j = exps.astype(F32)[:, None, None]
        mag_j = jnp.exp(j * lam_re)
        return mag_j * jnp.cos(j * lam_im), mag_j * jnp.sin(j * lam_im)

    lr, li = dup(a_re), dup(a_im)
    lam_re, lam_im = lr * dt, li * dt
    mag = jnp.exp(lam_re)
    ab_re, ab_im = mag * jnp.cos(lam_im), mag * jnp.sin(lam_im)
    den = lr * lr + li * li
    nr, ni = ab_re - 1.0, ab_im
    f_re = (nr * lr + ni * li) / den
    f_im = (ni * lr - nr * li) / den
    pr, pi = powers(t_len - 1 - jnp.arange(t_len), lam_re, lam_im)
    w_re = (pr * f_re - pi * f_im)[:, :, None, :]
    w_im = (pr * f_im + pi * f_re)[:, :, None, :]
    bt_re, bt_im = dup(b_re.transpose(0, 2, 1)), dup(b_im.transpose(0, 2, 1))
    wz_re = w_re * bt_re - w_im * bt_im
    wz_im = w_re * bt_im + w_im * bt_re
    qr, qi = powers(jnp.arange(t_len + 1), lam_re, lam_im)
    qr, qi = qr[:, :, None, :], qi[:, :, None, :]
    ct_re, ct_im = dup(c_re), dup(c_im)
    cv_re = ct_re * qr - ct_im * qi
    cv_im = -(ct_re * qi + ct_im * qr)

    def rows(x):
        return x.reshape(x.shape[0], g_cnt * SSM_GROUP, 2 * p_cnt)

    ap_re, ap_im = powers(t_len * jnp.arange(1, SUBLANES + 1), a_re * dt, a_im * dt)
    a_pow = jnp.stack([ap_re.reshape(SUBLANES, n_slab, -1), ap_im.reshape(SUBLANES, n_slab, -1)], axis=2)
    a_pow = a_pow.reshape(SUBLANES, n_slab * SLAB_STATE)
    d_til = jnp.tile(d_skip.reshape(n_slab, 1, LANES), (1, 1, t_len))
    return rows(wz_re), rows(wz_im), rows(cv_re), rows(cv_im), a_pow, d_til


def _s5_layer(h, g, w_in, a_re, a_im, log_dt, b_re, b_im, c_re, c_im, d_skip, w_glu, b_glu, w_out):
    seq, d = h.shape
    n_slab = d // LANES
    t_len = SSM_CHUNK
    n_chunk = seq // t_len
    tw = t_len * LANES
    sw = SLAB_STATE
    wz_re, wz_im, cv_re, cv_im, a_pow, d_til = _s5_derived(
        a_re, a_im, log_dt, b_re, b_im, c_re, c_im, d_skip)

    tm = ROW_TILE
    uc, z = pl.pallas_call(
        _s5_in_kernel,
        grid=(seq // tm,),
        in_specs=[pl.BlockSpec((tm, d), lambda i: (i, 0)),
                  pl.BlockSpec((1, d), lambda i: (0, 0)),
                  pl.BlockSpec((d, 2 * d), lambda i: (0, 0))],
        out_specs=[pl.BlockSpec((n_slab, tm // t_len, tw), lambda i: (0, i, 0)),
                   pl.BlockSpec((tm, d), lambda i: (i, 0))],
        out_shape=[jax.ShapeDtypeStruct((n_slab, n_chunk, tw), F32),
                   jax.ShapeDtypeStruct((seq, d), BF16)],
        scratch_shapes=[pltpu.VMEM((n_slab, tm, LANES), F32)],
        compiler_params=_cparams("parallel"),
        name="s5_in",
    )(h, g.reshape(1, d), w_in.astype(BF16))

    tr = CHUNK_TILE
    steps = lambda s, i: (0, s, 0)
    zst = pl.pallas_call(
        _ssm_state_kernel,
        grid=(n_slab, n_chunk // tr),
        in_specs=[pl.BlockSpec((None, tr, tw), lambda s, i: (s, i, 0)),
                  pl.BlockSpec((t_len, LANES, LANES), steps),
                  pl.BlockSpec((t_len, LANES, LANES), steps)],
        out_specs=pl.BlockSpec((tr, sw), lambda s, i: (i, s)),
        out_shape=jax.ShapeDtypeStruct((n_chunk, n_slab * sw), F32),
        scratch_shapes=[pltpu.VMEM((tw, sw), BF16)],
        compiler_params=_cparams("parallel", "arbitrary"),
        name="ssm_state",
    )(uc, wz_re, wz_im)

    tc = SCAN_TILE
    sprev = pl.pallas_call(
        _ssm_scan_kernel,
        grid=(n_chunk // tc,),
        in_specs=[pl.BlockSpec((tc, n_slab * sw), lambda i: (i, 0)),
                  pl.BlockSpec((SUBLANES, n_slab * sw), lambda i: (0, 0))],
        out_specs=pl.BlockSpec((tc, n_slab * sw), lambda i: (i, 0)),
        out_shape=jax.ShapeDtypeStruct((n_chunk, n_slab * sw), F32),
        scratch_shapes=[pltpu.VMEM((SUBLANES, n_slab * sw), F32)],
        compiler_params=_cparams("arbitrary"),
        name="ssm_scan",
    )(zst, a_pow)

    last_step = lambda s, i: (t_len - 1, s, 0)
    yg = pl.pallas_call(
        _ssm_out_kernel,
        grid=(n_slab, n_chunk // tr),
        in_specs=[pl.BlockSpec((None, tr, tw), lambda s, i: (s, i, 0)),
                  pl.BlockSpec((tr, sw), lambda s, i: (i, s)),
                  pl.BlockSpec((t_len + 1, LANES, LANES), steps),
                  pl.BlockSpec((t_len + 1, LANES, LANES), steps),
                  pl.BlockSpec((None, LANES, LANES), last_step),
                  pl.BlockSpec((None, LANES, LANES), last_step),
                  pl.BlockSpec((None, 1, tw), lambda s, i: (s, 0, 0))],
        out_specs=pl.BlockSpec((None, tr * t_len, LANES), lambda s, i: (s, i, 0)),
        out_shape=jax.ShapeDtypeStruct((n_slab, seq, LANES), F32),
        scratch_shapes=[pltpu.VMEM((tw, tw), BF16),
                        pltpu.VMEM((tw + LANES, sw), BF16),
                        pltpu.VMEM((tw, sw), BF16)],
        compiler_params=_cparams("parallel", "arbitrary"),
        name="ssm_out",
    )(uc, sprev, cv_re, cv_im, wz_re, wz_im, d_til)

    return pl.pallas_call(
        _s5_post_kernel,
        grid=(seq // tm,),
        in_specs=[pl.BlockSpec((n_slab, tm, LANES), lambda i: (0, i, 0)),
                  pl.BlockSpec((tm, d), lambda i: (i, 0)),
                  pl.BlockSpec((tm, d), lambda i: (i, 0)),
                  pl.BlockSpec((d, d), lambda i: (0, 0)),
                  pl.BlockSpec((1, d), lambda i: (0, 0)),
                  pl.BlockSpec((d, d), lambda i: (0, 0))],
        out_specs=pl.BlockSpec((tm, d), lambda i: (i, 0)),
        out_shape=jax.ShapeDtypeStruct((seq, d), F32),
        compiler_params=_cparams("parallel"),
        name="s5_post",
    )(yg, z, h, w_glu.astype(BF16), b_glu.reshape(1, d), w_out.astype(BF16))


def _attn_in_kernel(h_ref, g_ref, wqkz_ref, wvt_ref, qg_ref, kg_ref, cos_ref, sa_ref, sb_ref,
                    q_ref, ka_ref, km_ref, vt_ref, z_ref):
    tm, d = h_ref.shape
    n_head = d // HEAD_DIM
    n_blk = tm // MOBA_BLOCK
    i = pl.program_id(0)
    hn = _rms(h_ref[...], g_ref[...]).astype(BF16)
    qkz = jnp.dot(hn, wqkz_ref[...], preferred_element_type=F32)
    vt = lax.dot_general(wvt_ref[...], hn, _NT, preferred_element_type=F32)
    cosf, sa, sb = cos_ref[...], sa_ref[...], sb_ref[...]

    def norm_rope(t, gain):
        t = _rms(t, gain)
        up = pltpu.roll(t, HEAD_DIM - ROT_DIM // 2, 1)
        dn = pltpu.roll(t, ROT_DIM // 2, 1)
        return t * cosf + up * sa + dn * sb

    scale = HEAD_DIM ** -0.5 * LOG2_E
    for hd in range(n_head):
        qh = norm_rope(qkz[:, hd * HEAD_DIM:(hd + 1) * HEAD_DIM], qg_ref[...])
        q_ref[hd] = qh * scale
        kh = norm_rope(qkz[:, d + hd * HEAD_DIM:d + (hd + 1) * HEAD_DIM], kg_ref[...])
        ka_ref[hd, :, :HEAD_DIM] = kh.astype(BF16)
        for b in range(n_blk):
            km_ref[b, hd:hd + 1, :] = jnp.mean(
                kh[b * MOBA_BLOCK:(b + 1) * MOBA_BLOCK], axis=0, keepdims=True)
    lane = lax.broadcasted_iota(jnp.int32, (MOBA_BLOCK, LANES), 1)
    for b in range(n_blk):
        onehot = jnp.where(lane == i * n_blk + b, NEG_BIG, 0.0).astype(BF16)
        for hd in range(n_head):
            ka_ref[hd, b * MOBA_BLOCK:(b + 1) * MOBA_BLOCK, HEAD_DIM:] = onehot
        vt_ref[b] = vt[:, b * MOBA_BLOCK:(b + 1) * MOBA_BLOCK].astype(BF16)
    z_ref[...] = qkz[:, 2 * d:].astype(BF16)


def _moba_select_kernel(q_ref, km_ref, qa_ref):
    ts, hd = q_ref.shape
    n_blk = km_ref.shape[0]
    q = q_ref[...]
    gates = lax.dot_general(km_ref[...], q, _NT, precision=lax.Precision.HIGHEST,
                            preferred_element_type=F32)
    nidx = lax.broadcasted_iota(jnp.int32, gates.shape, 0)
    qpos = pl.program_id(1) * ts + lax.broadcasted_iota(jnp.int32, gates.shape, 1)
    own = qpos >> _log2(MOBA_BLOCK)
    valid = nidx < own
    g = jnp.where(valid, gates, -jnp.inf)
    picked = jnp.zeros(gates.shape, jnp.bool_)
    for _ in range(MOBA_TOPK):
        mx = jnp.max(g, axis=0, keepdims=True)
        first = jnp.min(jnp.where(g == mx, nidx, n_blk), axis=0, keepdims=True)
        hit = nidx == first
        picked = jnp.logical_or(picked, hit)
        g = jnp.where(hit, -jnp.inf, g)
    visible = jnp.logical_or(jnp.logical_and(valid, picked), nidx == own)
    qa_ref[:hd, :] = q.T.astype(BF16)
    qa_ref[hd:hd + n_blk, :] = jnp.where(visible, 0.0, 1.0).astype(BF16)
    qa_ref[hd + n_blk:, :] = jnp.zeros((qa_ref.shape[0] - hd - n_blk, ts), BF16)


def _moba_kernel(qa_ref, k_ref, vt_ref, o_ref, sa_ref, sb_ref, m_ref, l_ref, acc_ref):
    step = pl.program_id(1)
    tq = qa_ref.shape[1]
    gk = KV_GROUP * MOBA_BLOCK
    assert KV_GROUP % (tq // MOBA_BLOCK) == 0

    m_ref[...] = jnp.full(m_ref.shape, -jnp.inf, F32)
    l_ref[...] = jnp.zeros_like(l_ref)
    acc_ref[...] = jnp.zeros_like(acc_ref)

    def score(gi, dst_ref):
        kt = k_ref[pl.ds(pl.multiple_of(gi * gk, gk), gk), :]
        dst_ref[...] = jnp.dot(kt, qa_ref[...], preferred_element_type=F32)

    def rows(b):
        return slice(b * MOBA_BLOCK, (b + 1) * MOBA_BLOCK)

    def absorb(src_ref, gi, causal):
        if causal and tq == gk:
            tri = (lax.broadcasted_iota(jnp.int32, (MOBA_BLOCK, MOBA_BLOCK), 0)
                   <= lax.broadcasted_iota(jnp.int32, (MOBA_BLOCK, MOBA_BLOCK), 1))
            for b in range(KV_GROUP):
                src_ref[rows(b), rows(b)] = jnp.where(tri, src_ref[rows(b), rows(b)], -jnp.inf)
        elif causal:
            kpos = gi * gk + lax.broadcasted_iota(jnp.int32, src_ref.shape, 0)
            qpos = step * tq + lax.broadcasted_iota(jnp.int32, src_ref.shape, 1)
            src_ref[...] = jnp.where(kpos <= qpos, src_ref[...], -jnp.inf)
        m_old = m_ref[...]
        m_new = m_old
        for b in range(KV_GROUP):
            m_new = jnp.maximum(m_new, jnp.max(src_ref[rows(b), :], axis=0, keepdims=True))
        alpha = jnp.exp2(m_old - m_new)
        lsum = alpha * l_ref[...]
        pv = alpha * acc_ref[...]
        for b in range(KV_GROUP):
            p = jnp.exp2(src_ref[rows(b), :] - m_new)
            lsum = lsum + jnp.sum(p, axis=0, keepdims=True)
            pv = pv + jnp.dot(vt_ref[gi * KV_GROUP + b], p.astype(BF16), preferred_element_type=F32)
        l_ref[...] = lsum
        acc_ref[...] = pv
        m_ref[...] = m_new

    n_full = (step * (tq // MOBA_BLOCK)) // KV_GROUP
    score(0, sa_ref)

    def pair(i, carry):
        score(2 * i + 1, sb_ref)
        absorb(sa_ref, 2 * i, False)
        score(2 * i + 2, sa_ref)
        absorb(sb_ref, 2 * i + 1, False)
        return carry

    lax.fori_loop(0, n_full // 2, pair, 0)

    @pl.when(n_full % 2 == 1)
    def _():
        score(n_full, sb_ref)
        absorb(sa_ref, n_full - 1, False)
        absorb(sb_ref, n_full, True)

    @pl.when(n_full % 2 == 0)
    def _():
        absorb(sa_ref, n_full, True)

    o_ref[...] = (acc_ref[...] / l_ref[...]).T.astype(o_ref.dtype)


def _attn_post_kernel(o_ref, z_ref, h_ref, wo_ref, out_ref):
    gated = o_ref[...].astype(F32) * jax.nn.silu(z_ref[...].astype(F32))
    out_ref[...] = h_ref[...] + jnp.dot(gated.astype(BF16), wo_ref[...], preferred_element_type=F32)


def _rope_tables(seq):
    half = ROT_DIM // 2
    inv_freq = ROPE_THETA ** (-(np.arange(half, dtype=np.float64) * 2.0) / ROT_DIM)
    ang = np.arange(seq, dtype=np.float64)[:, None] * inv_freq[None, :]
    cos, sin = np.cos(ang), np.sin(ang)
    pad = HEAD_DIM - ROT_DIM
    zeros = np.zeros((seq, half))
    cosf = np.concatenate([cos, cos, np.ones((seq, pad))], axis=1)
    sa = np.concatenate([-sin, zeros, np.zeros((seq, pad))], axis=1)
    sb = np.concatenate([zeros, sin, np.zeros((seq, pad))], axis=1)
    return tuple(jnp.asarray(t, F32) for t in (cosf, sa, sb))


def _moba_layer(h, g, w_in, q_gain, k_gain, w_out, rope):
    seq, d = h.shape
    n_head = d // HEAD_DIM
    n_blk = seq // MOBA_BLOCK
    assert n_blk <= LANES and n_blk % KV_GROUP == 0 and seq % MOBA_BLOCK == 0
    cosf, sa, sb = rope
    w_qkz = jnp.concatenate([w_in[:, :2 * d], w_in[:, 3 * d:]], axis=1).astype(BF16)
    w_vt = w_in[:, 2 * d:3 * d].astype(BF16).T

    tm = ROW_TILE
    bpt = tm // MOBA_BLOCK
    row = lambda i: (i, 0)
    const = lambda i: (0, 0)
    q, kaug, kmean, vt, z = pl.pallas_call(
        _attn_in_kernel,
        grid=(seq // tm,),
        in_specs=[pl.BlockSpec((tm, d), row),
                  pl.BlockSpec((1, d), const),
                  pl.BlockSpec((d, 3 * d), const),
                  pl.BlockSpec((d, d), const),
                  pl.BlockSpec((1, HEAD_DIM), const),
                  pl.BlockSpec((1, HEAD_DIM), const),
                  pl.BlockSpec((tm, HEAD_DIM), row),
                  pl.BlockSpec((tm, HEAD_DIM), row),
                  pl.BlockSpec((tm, HEAD_DIM), row)],
        out_specs=[pl.BlockSpec((n_head, tm, HEAD_DIM), lambda i: (0, i, 0)),
                   pl.BlockSpec((n_head, tm, 2 * HEAD_DIM), lambda i: (0, i, 0)),
                   pl.BlockSpec((bpt, n_head, HEAD_DIM), lambda i: (i, 0, 0)),
                   pl.BlockSpec((bpt, d, MOBA_BLOCK), lambda i: (i, 0, 0)),
                   pl.BlockSpec((tm, d), row)],
        out_shape=[jax.ShapeDtypeStruct((n_head, seq, HEAD_DIM), F32),
                   jax.ShapeDtypeStruct((n_head, seq, 2 * HEAD_DIM), BF16),
                   jax.ShapeDtypeStruct((n_blk, n_head, HEAD_DIM), F32),
                   jax.ShapeDtypeStruct((n_blk, d, MOBA_BLOCK), BF16),
                   jax.ShapeDtypeStruct((seq, d), BF16)],
        compiler_params=_cparams("parallel"),
        name="attn_in",
    )(h, g.reshape(1, d), w_qkz, w_vt, q_gain.reshape(1, HEAD_DIM), k_gain.reshape(1, HEAD_DIM),
      cosf, sa, sb)

    qa = pl.pallas_call(
        _moba_select_kernel,
        grid=(n_head, seq // SEL_TILE),
        in_specs=[pl.BlockSpec((None, SEL_TILE, HEAD_DIM), lambda hd, i: (hd, i, 0)),
                  pl.BlockSpec((None, n_blk, HEAD_DIM), lambda hd, i: (hd, 0, 0))],
        out_specs=pl.BlockSpec((None, 2 * HEAD_DIM, SEL_TILE), lambda hd, i: (hd, 0, i)),
        out_shape=jax.ShapeDtypeStruct((n_head, 2 * HEAD_DIM, seq), BF16),
        compiler_params=_cparams("parallel", "parallel"),
        name="moba_select",
    )(q, kmean.transpose(1, 0, 2))

    o = pl.pallas_call(
        _moba_kernel,
        grid=(n_head, seq // Q_TILE),
        in_specs=[pl.BlockSpec((None, 2 * HEAD_DIM, Q_TILE), lambda hd, i: (hd, 0, i)),
                  pl.BlockSpec((None, seq, 2 * HEAD_DIM), lambda hd, i: (hd, 0, 0)),
                  pl.BlockSpec((n_blk, HEAD_DIM, MOBA_BLOCK), lambda hd, i: (0, hd, 0))],
        out_specs=pl.BlockSpec((Q_TILE, HEAD_DIM), lambda hd, i: (i, hd)),
        out_shape=jax.ShapeDtypeStruct((seq, d), BF16),
        scratch_shapes=[pltpu.VMEM((KV_GROUP * MOBA_BLOCK, Q_TILE), F32),
                        pltpu.VMEM((KV_GROUP * MOBA_BLOCK, Q_TILE), F32),
                        pltpu.VMEM((1, Q_TILE), F32),
                        pltpu.VMEM((1, Q_TILE), F32),
                        pltpu.VMEM((HEAD_DIM, Q_TILE), F32)],
        compiler_params=_cparams("parallel", "arbitrary"),
        name="moba_attn",
    )(qa, kaug, vt)

    return pl.pallas_call(
        _attn_post_kernel,
        grid=(seq // tm,),
        in_specs=[pl.BlockSpec((tm, d), row),
                  pl.BlockSpec((tm, d), row),
                  pl.BlockSpec((tm, d), row),
                  pl.BlockSpec((d, d), const)],
        out_specs=pl.BlockSpec((tm, d), row),
        out_shape=jax.ShapeDtypeStruct((seq, d), F32),
        compiler_params=_cparams("parallel"),
        name="attn_post",
    )(o, z, h, w_out.astype(BF16))


def kernel(x, norm_g, ssm_w_in, ssm_a_re, ssm_a_im, ssm_log_dt, ssm_b_re, ssm_b_im, ssm_c_re, ssm_c_im, ssm_d, ssm_w_glu, ssm_b_glu, ssm_w_out, attn_w_in, attn_q_gain, attn_k_gain, attn_w_out):
    bsz, seq, d = x.shape
    depth = norm_g.shape[0]
    rope = _rope_tables(seq)
    outs = []
    for b in range(bsz):
        h = x[b]
        for i in range(depth):
            j = i // 2
            if i % 2 == 0:
                h = _s5_layer(h, norm_g[i], ssm_w_in[j], ssm_a_re[j], ssm_a_im[j], ssm_log_dt[j],
                              ssm_b_re[j], ssm_b_im[j], ssm_c_re[j], ssm_c_im[j], ssm_d[j],
                              ssm_w_glu[j], ssm_b_glu[j], ssm_w_out[j])
            else:
                h = _moba_layer(h, norm_g[i], attn_w_in[j], attn_q_gain[j], attn_k_gain[j],
                                attn_w_out[j], rope)
        outs.append(h)
    return jnp.stack(outs)
```

```python
import jax
import jax.numpy as jnp
import numpy as np
from jax import lax
from jax.experimental import pallas as pl
from jax.experimental.pallas import tpu as pltpu

F32 = jnp.float32
BF16 = jnp.bfloat16

NORM_EPS = 1e-6
SSM_GROUP = 16
SSM_STATE = 64
HEAD_DIM = 128
ROT_DIM = HEAD_DIM // 4
ROPE_THETA = 500000.0
MOBA_BLOCK = 256
MOBA_TOPK = 3

LANES = 128
MXU_COLS = 256
SUBLANES = 8
SLAB_GROUPS = LANES // SSM_GROUP
SLAB_STATE = 2 * SLAB_GROUPS * SSM_STATE
SSM_CHUNK = 16
KV_GROUP = 4
NEG_BIG = -(2.0 ** 60)
LOG2_E = 1.4426950408889634
Q_TILE = 1024
V_ROWS = HEAD_DIM + 16
SEL_TILE = 2048
ROW_TILE = 512
CHUNK_TILE = 512
SCAN_TILE = 256
VMEM_LIMIT = 56 * 1024 * 1024

_NT = (((1,), (1,)), ((), ()))


def _cparams(*sem):
    return pltpu.CompilerParams(dimension_semantics=sem, vmem_limit_bytes=VMEM_LIMIT)


def _rms(x, g):
    ms = jnp.mean(x * x, axis=-1, keepdims=True)
    return x * lax.rsqrt(ms + NORM_EPS) * g


def _log2(n):
    assert n & (n - 1) == 0
    return n.bit_length() - 1


def _s5_in_kernel(h_ref, g_ref, w_ref, u_ref, z_ref, us_ref):
    tm, d = h_ref.shape
    n_row = tm // SSM_CHUNK
    hn = _rms(h_ref[...], g_ref[...]).astype(BF16)
    proj = jnp.dot(hn, w_ref[...], preferred_element_type=F32)
    z_ref[...] = proj[:, d:].astype(BF16)
    for s in range(d // LANES):
        us_ref[s] = proj[:, s * LANES:(s + 1) * LANES]
        for t in range(SSM_CHUNK):
            u_ref[s, :, t * LANES:(t + 1) * LANES] = us_ref[s, pl.ds(t, n_row, stride=SSM_CHUNK), :]


def _expand_rows(re2, im2):
    r = lax.broadcasted_iota(jnp.int32, re2.shape, 0)
    lane = lax.broadcasted_iota(jnp.int32, re2.shape, 1)
    row_group = r >> _log2(SSM_GROUP)
    lane_half = lane >> _log2(SSM_STATE)
    cols_re, cols_im = [], []
    for j in range(SLAB_STATE // 2 // LANES):
        msk = row_group == 2 * j + lane_half
        cols_re.append(jnp.where(msk, re2, 0.0))
        cols_im.append(jnp.where(msk, im2, 0.0))
    return jnp.concatenate(cols_re + cols_im, axis=1)


def _ssm_state_kernel(u_ref, re2_ref, im2_ref, z_ref, wz_ref):
    @pl.when(pl.program_id(1) == 0)
    def _():
        for k in range(re2_ref.shape[0]):
            wz_ref[k * LANES:(k + 1) * LANES, :] = _expand_rows(re2_ref[k], im2_ref[k]).astype(BF16)

    z_ref[...] = jnp.dot(u_ref[...].astype(BF16), wz_ref[...], preferred_element_type=F32)


def _ssm_scan_kernel(z_ref, ap_ref, o_ref, last_ref):
    half = SLAB_STATE // 2
    n_slab = z_ref.shape[1] // SLAB_STATE

    @pl.when(pl.program_id(0) == 0)
    def _():
        last_ref[...] = jnp.zeros_like(last_ref)

    rows = lax.broadcasted_iota(jnp.int32, (SUBLANES, half), 0)

    def shift_down(x, k, fill):
        return jnp.where(rows >= k, pltpu.roll(x, k, 0), fill)

    def body(step, _):
        r0 = pl.multiple_of(step * SUBLANES, SUBLANES)
        for s in range(n_slab):
            re_sl = slice(s * SLAB_STATE, s * SLAB_STATE + half)
            im_sl = slice(s * SLAB_STATE + half, (s + 1) * SLAB_STATE)
            xr = z_ref[pl.ds(r0, SUBLANES), re_sl]
            xi = z_ref[pl.ds(r0, SUBLANES), im_sl]
            for k in (1, 2, 4):
                ar = ap_ref[k - 1:k, re_sl]
                ai = ap_ref[k - 1:k, im_sl]
                sr, si = shift_down(xr, k, 0.0), shift_down(xi, k, 0.0)
                xr, xi = xr + ar * sr - ai * si, xi + ar * si + ai * sr
            cr = jnp.broadcast_to(last_ref[SUBLANES - 1:SUBLANES, re_sl], (SUBLANES, half))
            ci = jnp.broadcast_to(last_ref[SUBLANES - 1:SUBLANES, im_sl], (SUBLANES, half))
            apr, api = ap_ref[:, re_sl], ap_ref[:, im_sl]
            xr, xi = xr + apr * cr - api * ci, xi + apr * ci + api * cr
            o_ref[pl.ds(r0, SUBLANES), re_sl] = shift_down(xr, 1, cr)
            o_ref[pl.ds(r0, SUBLANES), im_sl] = shift_down(xi, 1, ci)
            last_ref[:, re_sl] = xr
            last_ref[:, im_sl] = xi
        return 0

    lax.fori_loop(0, z_ref.shape[0] // SUBLANES, body, 0)


def _ssm_out_kernel(u_ref, sp_ref, cre_ref, cim_ref, fre_ref, fim_ref, d_ref, y_ref,
                    m_ref, ct_ref, clo_ref):
    t_len = SSM_CHUNK
    tr = u_ref.shape[0]
    tw = t_len * LANES

    @pl.when(pl.program_id(1) == 0)
    def _():
        for t in range(t_len + 1):
            x = _expand_rows(cre_ref[t], cim_ref[t])
            hi = x.astype(BF16)
            ct_ref[t * LANES:(t + 1) * LANES, :] = hi
            if t < t_len:
                clo_ref[t * LANES:(t + 1) * LANES, :] = (x - hi.astype(F32)).astype(BF16)
        fb = _expand_rows(fre_ref[...], fim_ref[...])
        fb_hi = fb.astype(BF16)
        fb_lo = (fb - fb_hi.astype(F32)).astype(BF16)
        c_hi, c_lo = ct_ref[:tw, :], clo_ref[...]
        krow = (lax.dot_general(fb_hi, c_hi, _NT, preferred_element_type=F32)
                + lax.dot_general(fb_lo, c_hi, _NT, preferred_element_type=F32)
                + lax.dot_general(fb_hi, c_lo, _NT, preferred_element_type=F32)).astype(BF16)
        for k in range(t_len):
            parts = [krow[:, :(t_len - k) * LANES]]
            if k:
                parts = [jnp.zeros((LANES, k * LANES), BF16)] + parts
            m_ref[k * LANES:(k + 1) * LANES, :] = jnp.concatenate(parts, axis=1)

    u = u_ref[...]
    ub = u.astype(BF16)
    sp = sp_ref[...].astype(BF16)
    for j in range(t_len // 2):
        lo, hi = 2 * j * LANES, (2 * j + 2) * LANES
        y = jnp.dot(ub[:, :hi], m_ref[:hi, lo:hi], preferred_element_type=F32)
        y = y + lax.dot_general(sp, ct_ref[LANES + lo:LANES + hi, :], _NT, preferred_element_type=F32)
        y = jax.nn.gelu(y + d_ref[:, lo:hi] * u[:, lo:hi], approximate=True)
        for t in (2 * j, 2 * j + 1):
            y_ref[pl.ds(t, tr, stride=t_len), :] = y[:, t * LANES - lo:(t + 1) * LANES - lo]


def _s5_post_kernel(y_ref, z_ref, h_ref, wg_ref, bg_ref, wo_ref, o_ref):
    y = jnp.concatenate([y_ref[s] for s in range(y_ref.shape[0])], axis=-1)
    lin = jnp.dot(y.astype(BF16), wg_ref[...], preferred_element_type=F32) + bg_ref[...]
    gated = y * jax.nn.sigmoid(lin) * jax.nn.silu(z_ref[...].astype(F32))
    o_ref[...] = h_ref[...] + jnp.dot(gated.astype(BF16), wo_ref[...], preferred_element_type=F32)


def _s5_derived(a_re, a_im, log_dt, b_re, b_im, c_re, c_im, d_skip):
    t_len = SSM_CHUNK
    g_cnt, p_cnt = a_re.shape
    n_slab = g_cnt // SLAB_GROUPS
    dt = jnp.exp(log_dt)[:, None]

    def dup(x):
        return jnp.concatenate([x, x], axis=-1)

    def powers(exps, lam_re, lam_im):
        j = exps.astype(F32)[:, None, None]
        mag_j = jnp.exp(j * lam_re)
        return mag_j * jnp.cos(j * lam_im), mag_j * jnp.sin(j * lam_im)

    lr, li = dup(a_re), dup(a_im)
    lam_re, lam_im = lr * dt, li * dt
    mag = jnp.exp(lam_re)
    ab_re, ab_im = mag * jnp.cos(lam_im), mag * jnp.sin(lam_im)
    den = lr * lr + li * li
    nr, ni = ab_re - 1.0, ab_im
    f_re = (nr * lr + ni * li) / den
    f_im = (ni * lr - nr * li) / den
    pr, pi = powers(t_len - 1 - jnp.arange(t_len), lam_re, lam_im)
    w_re = (pr * f_re - pi * f_im)[:, :, None, :]
    w_im = (pr * f_im + pi * f_re)[:, :, None, :]
    bt_re, bt_im = dup(b_re.transpose(0, 2, 1)), dup(b_im.transpose(0, 2, 1))
    wz_re = w_re * bt_re - w_im * bt_im
    wz_im = w_re * bt_im + w_im * bt_re
    qr, qi = powers(jnp.arange(t_len + 1), lam_re, lam_im)
    qr, qi = qr[:, :, None, :], qi[:, :, None, :]
    ct_re, ct_im = dup(c_re), dup(c_im)
    cv_re = ct_re * qr - ct_im * qi
    cv_im = -(ct_re * qi + ct_im * qr)

    def rows(x):
        return x.reshape(x.shape[0], g_cnt * SSM_GROUP, 2 * p_cnt)

    ap_re, ap_im = powers(t_len * jnp.arange(1, SUBLANES + 1), a_re * dt, a_im * dt)
    a_pow = jnp.stack([ap_re.reshape(SUBLANES, n_slab, -1), ap_im.reshape(SUBLANES, n_slab, -1)], axis=2)
    a_pow = a_pow.reshape(SUBLANES, n_slab * SLAB_STATE)
    d_til = jnp.tile(d_skip.reshape(n_slab, 1, LANES), (1, 1, t_len))
    return rows(wz_re), rows(wz_im), rows(cv_re), rows(cv_im), a_pow, d_til


def _s5_layer(h, g, w_in, a_re, a_im, log_dt, b_re, b_im, c_re, c_im, d_skip, w_glu, b_glu, w_out):
    seq, d = h.shape
    n_slab = d // LANES
    t_len = SSM_CHUNK
    n_chunk = seq // t_len
    tw = t_len * LANES
    sw = SLAB_STATE
    wz_re, wz_im, cv_re, cv_im, a_pow, d_til = _s5_derived(
        a_re, a_im, log_dt, b_re, b_im, c_re, c_im, d_skip)

    tm = ROW_TILE
    uc, z = pl.pallas_call(
        _s5_in_kernel,
        grid=(seq // tm,),
        in_specs=[pl.BlockSpec((tm, d), lambda i: (i, 0)),
                  pl.BlockSpec((1, d), lambda i: (0, 0)),
                  pl.BlockSpec((d, 2 * d), lambda i: (0, 0))],
        out_specs=[pl.BlockSpec((n_slab, tm // t_len, tw), lambda i: (0, i, 0)),
                   pl.BlockSpec((tm, d), lambda i: (i, 0))],
        out_shape=[jax.ShapeDtypeStruct((n_slab, n_chunk, tw), F32),
                   jax.ShapeDtypeStruct((seq, d), BF16)],
        scratch_shapes=[pltpu.VMEM((n_slab, tm, LANES), F32)],
        compiler_params=_cparams("parallel"),
        name="s5_in",
    )(h, g.reshape(1, d), w_in.astype(BF16))

    tr = CHUNK_TILE
    steps = lambda s, i: (0, s, 0)
    zst = pl.pallas_call(
        _ssm_state_kernel,
        grid=(n_slab, n_chunk // tr),
        in_specs=[pl.BlockSpec((None, tr, tw), lambda s, i: (s, i, 0)),
                  pl.BlockSpec((t_len, LANES, LANES), steps),
                  pl.BlockSpec((t_len, LANES, LANES), steps)],
        out_specs=pl.BlockSpec((tr, sw), lambda s, i: (i, s)),
        out_shape=jax.ShapeDtypeStruct((n_chunk, n_slab * sw), F32),
        scratch_shapes=[pltpu.VMEM((tw, sw), BF16)],
        compiler_params=_cparams("parallel", "arbitrary"),
        name="ssm_state",
    )(uc, wz_re, wz_im)

    tc = SCAN_TILE
    sprev = pl.pallas_call(
        _ssm_scan_kernel,
        grid=(n_chunk // tc,),
        in_specs=[pl.BlockSpec((tc, n_slab * sw), lambda i: (i, 0)),
                  pl.BlockSpec((SUBLANES, n_slab * sw), lambda i: (0, 0))],
        out_specs=pl.BlockSpec((tc, n_slab * sw), lambda i: (i, 0)),
        out_shape=jax.ShapeDtypeStruct((n_chunk, n_slab * sw), F32),
        scratch_shapes=[pltpu.VMEM((SUBLANES, n_slab * sw), F32)],
        compiler_params=_cparams("arbitrary"),
        name="ssm_scan",
    )(zst, a_pow)

    last_step = lambda s, i: (t_len - 1, s, 0)
    yg = pl.pallas_call(
        _ssm_out_kernel,
        grid=(n_slab, n_chunk // tr),
        in_specs=[pl.BlockSpec((None, tr, tw), lambda s, i: (s, i, 0)),
                  pl.BlockSpec((tr, sw), lambda s, i: (i, s)),
                  pl.BlockSpec((t_len + 1, LANES, LANES), steps),
                  pl.BlockSpec((t_len + 1, LANES, LANES), steps),
                  pl.BlockSpec((None, LANES, LANES), last_step),
                  pl.BlockSpec((None, LANES, LANES), last_step),
                  pl.BlockSpec((None, 1, tw), lambda s, i: (s, 0, 0))],
        out_specs=pl.BlockSpec((None, tr * t_len, LANES), lambda s, i: (s, i, 0)),
        out_shape=jax.ShapeDtypeStruct((n_slab, seq, LANES), F32),
        scratch_shapes=[pltpu.VMEM((tw, tw), BF16),
                        pltpu.VMEM((tw + LANES, sw), BF16),
                        pltpu.VMEM((tw, sw), BF16)],
        compiler_params=_cparams("parallel", "arbitrary"),
        name="ssm_out",
    )(uc, sprev, cv_re, cv_im, wz_re, wz_im, d_til)

    return pl.pallas_call(
        _s5_post_kernel,
        grid=(seq // tm,),
        in_specs=[pl.BlockSpec((n_slab, tm, LANES), lambda i: (0, i, 0)),
                  pl.BlockSpec((tm, d), lambda i: (i, 0)),
                  pl.BlockSpec((tm, d), lambda i: (i, 0)),
                  pl.BlockSpec((d, d), lambda i: (0, 0)),
                  pl.BlockSpec((1, d), lambda i: (0, 0)),
                  pl.BlockSpec((d, d), lambda i: (0, 0))],
        out_specs=pl.BlockSpec((tm, d), lambda i: (i, 0)),
        out_shape=jax.ShapeDtypeStruct((seq, d), F32),
        compiler_params=_cparams("parallel"),
        name="s5_post",
    )(yg, z, h, w_glu.astype(BF16), b_glu.reshape(1, d), w_out.astype(BF16))


def _attn_in_kernel(h_ref, g_ref, wqkz_ref, wvt_ref, qg_ref, kg_ref, cos_ref, sa_ref, sb_ref,
                    q_ref, ka_ref, km_ref, vt_ref, z_ref):
    tm, d = h_ref.shape
    n_head = d // HEAD_DIM
    n_blk = tm // MOBA_BLOCK
    i = pl.program_id(0)
    hn = _rms(h_ref[...], g_ref[...]).astype(BF16)
    cosf, sa, sb = cos_ref[...], sa_ref[...], sb_ref[...]

    def proj(col, width):
        return jnp.dot(hn, wqkz_ref[:, col:col + width], preferred_element_type=F32)

    def norm_rope(t, gain):
        t = _rms(t, gain)
        up = pltpu.roll(t, HEAD_DIM - ROT_DIM // 2, 1)
        dn = pltpu.roll(t, ROT_DIM // 2, 1)
        return t * cosf + up * sa + dn * sb

    scale = HEAD_DIM ** -0.5 * LOG2_E
    pairw = 2 * HEAD_DIM
    for hp in range(n_head // 2):
        qp = proj(hp * pairw, pairw)
        kp = proj(d + hp * pairw, pairw)
        for sub in range(2):
            hd = 2 * hp + sub
            qh = norm_rope(qp[:, sub * HEAD_DIM:(sub + 1) * HEAD_DIM], qg_ref[...])
            q_ref[hd] = qh * scale
            kh = norm_rope(kp[:, sub * HEAD_DIM:(sub + 1) * HEAD_DIM], kg_ref[...])
            ka_ref[hd, :, :HEAD_DIM] = kh.astype(BF16)
            for b in range(n_blk):
                km_ref[b, hd:hd + 1, :] = jnp.mean(
                    kh[b * MOBA_BLOCK:(b + 1) * MOBA_BLOCK], axis=0, keepdims=True)
    vt = lax.dot_general(wvt_ref[...], hn, _NT, preferred_element_type=F32)
    lane = lax.broadcasted_iota(jnp.int32, (MOBA_BLOCK, LANES), 1)
    pad_row = lax.broadcasted_iota(jnp.int32, (V_ROWS - HEAD_DIM, MOBA_BLOCK), 0)
    ones_row = jnp.where(pad_row == 0, 1.0, 0.0).astype(BF16)
    for b in range(n_blk):
        onehot = jnp.where(lane == i * n_blk + b, NEG_BIG, 0.0).astype(BF16)
        for hd in range(n_head):
            ka_ref[hd, b * MOBA_BLOCK:(b + 1) * MOBA_BLOCK, HEAD_DIM:] = onehot
        for hd in range(n_head):
            vt_ref[b, hd * V_ROWS:hd * V_ROWS + HEAD_DIM, :] = vt[
                hd * HEAD_DIM:(hd + 1) * HEAD_DIM, b * MOBA_BLOCK:(b + 1) * MOBA_BLOCK].astype(BF16)
            vt_ref[b, hd * V_ROWS + HEAD_DIM:(hd + 1) * V_ROWS, :] = ones_row
    z_ref[...] = proj(2 * d, d).astype(BF16)


def _moba_select_kernel(q_ref, km_ref, qa_ref):
    ts, hd = q_ref.shape
    n_blk = km_ref.shape[0]
    q = q_ref[...]
    gates = lax.dot_general(km_ref[...], q, _NT, precision=lax.Precision.HIGHEST,
                            preferred_element_type=F32)
    nidx = lax.broadcasted_iota(jnp.int32, gates.shape, 0)
    qpos = pl.program_id(1) * ts + lax.broadcasted_iota(jnp.int32, gates.shape, 1)
    own = qpos >> _log2(MOBA_BLOCK)
    valid = nidx < own
    g = jnp.where(valid, gates, -jnp.inf)
    picked = jnp.zeros(gates.shape, jnp.bool_)
    for _ in range(MOBA_TOPK):
        mx = jnp.max(g, axis=0, keepdims=True)
        first = jnp.min(jnp.where(g == mx, nidx, n_blk), axis=0, keepdims=True)
        hit = nidx == first
        picked = jnp.logical_or(picked, hit)
        g = jnp.where(hit, -jnp.inf, g)
    visible = jnp.logical_or(jnp.logical_and(valid, picked), nidx == own)
    qa_ref[:hd, :] = q.T.astype(BF16)
    qa_ref[hd:hd + n_blk, :] = jnp.where(visible, 0.0, 1.0).astype(BF16)
    qa_ref[hd + n_blk:, :] = jnp.zeros((qa_ref.shape[0] - hd - n_blk, ts), BF16)


def _moba_kernel(qa_ref, k_ref, vt_ref, o_ref, sa_ref, sb_ref, m_ref, acc_ref):
    step = pl.program_id(1)
    tq = qa_ref.shape[1]
    gk = KV_GROUP * MOBA_BLOCK
    assert KV_GROUP % (tq // MOBA_BLOCK) == 0

    m_ref[...] = jnp.full(m_ref.shape, -jnp.inf, F32)
    acc_ref[...] = jnp.zeros_like(acc_ref)

    def score(gi, dst_ref):
        kt = k_ref[pl.ds(pl.multiple_of(gi * gk, gk), gk), :]
        dst_ref[...] = jnp.dot(kt, qa_ref[...], preferred_element_type=F32)

    def rows(b):
        return slice(b * MOBA_BLOCK, (b + 1) * MOBA_BLOCK)

    def absorb(src_ref, gi, causal):
        if causal and tq == gk:
            tri = (lax.broadcasted_iota(jnp.int32, (MOBA_BLOCK, MOBA_BLOCK), 0)
                   <= lax.broadcasted_iota(jnp.int32, (MOBA_BLOCK, MOBA_BLOCK), 1))
            for b in range(KV_GROUP):
                src_ref[rows(b), rows(b)] = jnp.where(tri, src_ref[rows(b), rows(b)], -jnp.inf)
        elif causal:
            kpos = gi * gk + lax.broadcasted_iota(jnp.int32, src_ref.shape, 0)
            qpos = step * tq + lax.broadcasted_iota(jnp.int32, src_ref.shape, 1)
            src_ref[...] = jnp.where(kpos <= qpos, src_ref[...], -jnp.inf)
        m_old = m_ref[...]
        m_new = m_old
        for b in range(KV_GROUP):
            m_new = jnp.maximum(m_new, jnp.max(src_ref[rows(b), :], axis=0, keepdims=True))
        alpha = jnp.exp2(m_old - m_new)
        pv = alpha * acc_ref[...]
        for b in range(KV_GROUP):
            p = jnp.exp2(src_ref[rows(b), :] - m_new)
            pv = pv + jnp.dot(vt_ref[gi * KV_GROUP + b], p.astype(BF16), preferred_element_type=F32)
        acc_ref[...] = pv
        m_ref[...] = m_new

    n_full = (step * (tq // MOBA_BLOCK)) // KV_GROUP
    score(0, sa_ref)

    def pair(i, carry):
        score(2 * i + 1, sb_ref)
        absorb(sa_ref, 2 * i, False)
        score(2 * i + 2, sa_ref)
        absorb(sb_ref, 2 * i + 1, False)
        return carry

    lax.fori_loop(0, n_full // 2, pair, 0)

    @pl.when(n_full % 2 == 1)
    def _():
        score(n_full, sb_ref)
        absorb(sa_ref, n_full - 1, False)
        absorb(sb_ref, n_full, True)

    @pl.when(n_full % 2 == 0)
    def _():
        absorb(sa_ref, n_full, True)

    o_ref[...] = (acc_ref[:HEAD_DIM, :] / acc_ref[HEAD_DIM:HEAD_DIM + 1, :]).T.astype(o_ref.dtype)


def _attn_post_kernel(o_ref, z_ref, h_ref, wo_ref, out_ref):
    out = h_ref[...]
    for c in range(o_ref.shape[1] // MXU_COLS):
        cols = slice(c * MXU_COLS, (c + 1) * MXU_COLS)
        gated = o_ref[:, cols].astype(F32) * jax.nn.silu(z_ref[:, cols].astype(F32))
        out = out + jnp.dot(gated.astype(BF16), wo_ref[cols, :], preferred_element_type=F32)
    out_ref[...] = out


def _rope_tables(seq):
    half = ROT_DIM // 2
    inv_freq = ROPE_THETA ** (-(np.arange(half, dtype=np.float64) * 2.0) / ROT_DIM)
    ang = np.arange(seq, dtype=np.float64)[:, None] * inv_freq[None, :]
    cos, sin = np.cos(ang), np.sin(ang)
    pad = HEAD_DIM - ROT_DIM
    zeros = np.zeros((seq, half))
    cosf = np.concatenate([cos, cos, np.ones((seq, pad))], axis=1)
    sa = np.concatenate([-sin, zeros, np.zeros((seq, pad))], axis=1)
    sb = np.concatenate([zeros, sin, np.zeros((seq, pad))], axis=1)
    return tuple(jnp.asarray(t, F32) for t in (cosf, sa, sb))


def _moba_layer(h, g, w_in, q_gain, k_gain, w_out, rope):
    seq, d = h.shape
    n_head = d // HEAD_DIM
    n_blk = seq // MOBA_BLOCK
    assert n_blk <= LANES and n_blk % KV_GROUP == 0 and seq % MOBA_BLOCK == 0
    cosf, sa, sb = rope
    w_qkz = jnp.concatenate([w_in[:, :2 * d], w_in[:, 3 * d:]], axis=1).astype(BF16)
    w_vt = w_in[:, 2 * d:3 * d].astype(BF16).T

    tm = ROW_TILE
    bpt = tm // MOBA_BLOCK
    row = lambda i: (i, 0)
    const = lambda i: (0, 0)
    q, kaug, kmean, vt, z = pl.pallas_call(
        _attn_in_kernel,
        grid=(seq // tm,),
        in_specs=[pl.BlockSpec((tm, d), row),
                  pl.BlockSpec((1, d), const),
                  pl.BlockSpec((d, 3 * d), const),
                  pl.BlockSpec((d, d), const),
                  pl.BlockSpec((1, HEAD_DIM), const),
                  pl.BlockSpec((1, HEAD_DIM), const),
                  pl.BlockSpec((tm, HEAD_DIM), row),
                  pl.BlockSpec((tm, HEAD_DIM), row),
                  pl.BlockSpec((tm, HEAD_DIM), row)],
        out_specs=[pl.BlockSpec((n_head, tm, HEAD_DIM), lambda i: (0, i, 0)),
                   pl.BlockSpec((n_head, tm, 2 * HEAD_DIM), lambda i: (0, i, 0)),
                   pl.BlockSpec((bpt, n_head, HEAD_DIM), lambda i: (i, 0, 0)),
                   pl.BlockSpec((bpt, n_head * V_ROWS, MOBA_BLOCK), lambda i: (i, 0, 0)),
                   pl.BlockSpec((tm, d), row)],
        out_shape=[jax.ShapeDtypeStruct((n_head, seq, HEAD_DIM), F32),
                   jax.ShapeDtypeStruct((n_head, seq, 2 * HEAD_DIM), BF16),
                   jax.ShapeDtypeStruct((n_blk, n_head, HEAD_DIM), F32),
                   jax.ShapeDtypeStruct((n_blk, n_head * V_ROWS, MOBA_BLOCK), BF16),
                   jax.ShapeDtypeStruct((seq, d), BF16)],
        compiler_params=_cparams("parallel"),
        name="attn_in",
    )(h, g.reshape(1, d), w_qkz, w_vt, q_gain.reshape(1, HEAD_DIM), k_gain.reshape(1, HEAD_DIM),
      cosf, sa, sb)

    qa = pl.pallas_call(
        _moba_select_kernel,
        grid=(n_head, seq // SEL_TILE),
        in_specs=[pl.BlockSpec((None, SEL_TILE, HEAD_DIM), lambda hd, i: (hd, i, 0)),
                  pl.BlockSpec((None, n_blk, HEAD_DIM), lambda hd, i: (hd, 0, 0))],
        out_specs=pl.BlockSpec((None, 2 * HEAD_DIM, SEL_TILE), lambda hd, i: (hd, 0, i)),
        out_shape=jax.ShapeDtypeStruct((n_head, 2 * HEAD_DIM, seq), BF16),
        compiler_params=_cparams("parallel", "parallel"),
        name="moba_select",
    )(q, kmean.transpose(1, 0, 2))

    o = pl.pallas_call(
        _moba_kernel,
        grid=(n_head, seq // Q_TILE),
        in_specs=[pl.BlockSpec((None, 2 * HEAD_DIM, Q_TILE), lambda hd, i: (hd, 0, i)),
                  pl.BlockSpec((None, seq, 2 * HEAD_DIM), lambda hd, i: (hd, 0, 0)),
                  pl.BlockSpec((n_blk, V_ROWS, MOBA_BLOCK), lambda hd, i: (0, hd, 0))],
        out_specs=pl.BlockSpec((Q_TILE, HEAD_DIM), lambda hd, i: (i, hd)),
        out_shape=jax.ShapeDtypeStruct((seq, d), BF16),
        scratch_shapes=[pltpu.VMEM((KV_GROUP * MOBA_BLOCK, Q_TILE), F32),
                        pltpu.VMEM((KV_GROUP * MOBA_BLOCK, Q_TILE), F32),
                        pltpu.VMEM((1, Q_TILE), F32),
                        pltpu.VMEM((V_ROWS, Q_TILE), F32)],
        compiler_params=_cparams("parallel", "arbitrary"),
        name="moba_attn",
    )(qa, kaug, vt)

    return pl.pallas_call(
        _attn_post_kernel,
        grid=(seq // tm,),
        in_specs=[pl.BlockSpec((tm, d), row),
                  pl.BlockSpec((tm, d), row),
                  pl.BlockSpec((tm, d), row),
                  pl.BlockSpec((d, d), const)],
        out_specs=pl.BlockSpec((tm, d), row),
        out_shape=jax.ShapeDtypeStruct((seq, d), F32),
        compiler_params=_cparams("parallel"),
        name="attn_post",
    )(o, z, h, w_out.astype(BF16))


def kernel(x, norm_g, ssm_w_in, ssm_a_re, ssm_a_im, ssm_log_dt, ssm_b_re, ssm_b_im, ssm_c_re, ssm_c_im, ssm_d, ssm_w_glu, ssm_b_glu, ssm_w_out, attn_w_in, attn_q_gain, attn_k_gain, attn_w_out):
    bsz, seq, d = x.shape
    depth = norm_g.shape[0]
    rope = _rope_tables(seq)
    outs = []
    for b in range(bsz):
        h = x[b]
        for i in range(depth):
            j = i // 2
            if i % 2 == 0:
                h = _s5_layer(h, norm_g[i], ssm_w_in[j], ssm_a_re[j], ssm_a_im[j], ssm_log_dt[j],
                              ssm_b_re[j], ssm_b_im[j], ssm_c_re[j], ssm_c_im[j], ssm_d[j],
                              ssm_w_glu[j], ssm_b_glu[j], ssm_w_out[j])
            else:
                h = _moba_layer(h, norm_g[i], attn_w_in[j], attn_q_gain[j], attn_k_gain[j],
                                attn_w_out[j], rope)
        outs.append(h)
    return jnp.stack(outs)
```

```python
import jax
import jax.numpy as jnp
import numpy as np
from jax import lax
from jax.experimental import pallas as pl
from jax.experimental.pallas import tpu as pltpu

F32 = jnp.float32
BF16 = jnp.bfloat16

NORM_EPS = 1e-6
SSM_GROUP = 16
SSM_STATE = 64
HEAD_DIM = 128
ROT_DIM = HEAD_DIM // 4
ROPE_THETA = 500000.0
MOBA_BLOCK = 256
MOBA_TOPK = 3

LANES = 128
MXU_COLS = 256
SUBLANES = 8
SLAB_GROUPS = LANES // SSM_GROUP
SLAB_STATE = 2 * SLAB_GROUPS * SSM_STATE
SSM_CHUNK = 16
KV_GROUP = 4
NEG_BIG = -(2.0 ** 60)
LOG2_E = 1.4426950408889634
Q_TILE = 1024
V_ROWS = HEAD_DIM + 16
SEL_TILE = 2048
ROW_TILE = 512
CHUNK_TILE = 512
SCAN_TILE = 256
VMEM_LIMIT = 56 * 1024 * 1024

_NT = (((1,), (1,)), ((), ()))


def _cparams(*sem):
    return pltpu.CompilerParams(dimension_semantics=sem, vmem_limit_bytes=VMEM_LIMIT)


def _rms(x, g):
    ms = jnp.mean(x * x, axis=-1, keepdims=True)
    return x * lax.rsqrt(ms + NORM_EPS) * g


def _log2(n):
    assert n & (n - 1) == 0
    return n.bit_length() - 1


def _s5_in_kernel(h_ref, g_ref, w_ref, u_ref, z_ref, us_ref):
    tm, d = h_ref.shape
    n_row = tm // SSM_CHUNK
    hn = _rms(h_ref[...], g_ref[...]).astype(BF16)
    proj = jnp.dot(hn, w_ref[...], preferred_element_type=F32)
    z_ref[...] = proj[:, d:].astype(BF16)
    for s in range(d // LANES):
        us_ref[s] = proj[:, s * LANES:(s + 1) * LANES]
        for t in range(SSM_CHUNK):
            u_ref[s, :, t * LANES:(t + 1) * LANES] = us_ref[s, pl.ds(t, n_row, stride=SSM_CHUNK), :]


def _expand_rows(re2, im2):
    r = lax.broadcasted_iota(jnp.int32, re2.shape, 0)
    lane = lax.broadcasted_iota(jnp.int32, re2.shape, 1)
    row_group = r >> _log2(SSM_GROUP)
    lane_half = lane >> _log2(SSM_STATE)
    cols_re, cols_im = [], []
    for j in range(SLAB_STATE // 2 // LANES):
        msk = row_group == 2 * j + lane_half
        cols_re.append(jnp.where(msk, re2, 0.0))
        cols_im.append(jnp.where(msk, im2, 0.0))
    return jnp.concatenate(cols_re + cols_im, axis=1)


def _ssm_state_kernel(u_ref, re2_ref, im2_ref, z_ref, wz_ref):
    @pl.when(pl.program_id(1) == 0)
    def _():
        for k in range(re2_ref.shape[0]):
            wz_ref[k * LANES:(k + 1) * LANES, :] = _expand_rows(re2_ref[k], im2_ref[k]).astype(BF16)

    z_ref[...] = jnp.dot(u_ref[...].astype(BF16), wz_ref[...], preferred_element_type=F32)


def _ssm_scan_kernel(z_ref, ap_ref, o_ref, last_ref):
    half = SLAB_STATE // 2
    n_slab = z_ref.shape[1] // SLAB_STATE

    @pl.when(pl.program_id(0) == 0)
    def _():
        last_ref[...] = jnp.zeros_like(last_ref)

    rows = lax.broadcasted_iota(jnp.int32, (SUBLANES, half), 0)

    def shift_down(x, k, fill):
        return jnp.where(rows >= k, pltpu.roll(x, k, 0), fill)

    def body(step, _):
        r0 = pl.multiple_of(step * SUBLANES, SUBLANES)
        for s in range(n_slab):
            re_sl = slice(s * SLAB_STATE, s * SLAB_STATE + half)
            im_sl = slice(s * SLAB_STATE + half, (s + 1) * SLAB_STATE)
            xr = z_ref[pl.ds(r0, SUBLANES), re_sl]
            xi = z_ref[pl.ds(r0, SUBLANES), im_sl]
            for k in (1, 2, 4):
                ar = ap_ref[k - 1:k, re_sl]
                ai = ap_ref[k - 1:k, im_sl]
                sr, si = shift_down(xr, k, 0.0), shift_down(xi, k, 0.0)
                xr, xi = xr + ar * sr - ai * si, xi + ar * si + ai * sr
            cr = jnp.broadcast_to(last_ref[SUBLANES - 1:SUBLANES, re_sl], (SUBLANES, half))
            ci = jnp.broadcast_to(last_ref[SUBLANES - 1:SUBLANES, im_sl], (SUBLANES, half))
            apr, api = ap_ref[:, re_sl], ap_ref[:, im_sl]
            xr, xi = xr + apr * cr - api * ci, xi + apr * ci + api * cr
            o_ref[pl.ds(r0, SUBLANES), re_sl] = shift_down(xr, 1, cr)
            o_ref[pl.ds(r0, SUBLANES), im_sl] = shift_down(xi, 1, ci)
            last_ref[:, re_sl] = xr
            last_ref[:, im_sl] = xi
        return 0

    lax.fori_loop(0, z_ref.shape[0] // SUBLANES, body, 0)


def _ssm_out_kernel(u_ref, sp_ref, cre_ref, cim_ref, fre_ref, fim_ref, d_ref, y_ref,
                    m_ref, ct_ref, clo_ref):
    t_len = SSM_CHUNK
    tr = u_ref.shape[0]
    tw = t_len * LANES

    @pl.when(pl.program_id(1) == 0)
    def _():
        for t in range(t_len + 1):
            x = _expand_rows(cre_ref[t], cim_ref[t])
            hi = x.astype(BF16)
            ct_ref[t * LANES:(t + 1) * LANES, :] = hi
            if t < t_len:
                clo_ref[t * LANES:(t + 1) * LANES, :] = (x - hi.astype(F32)).astype(BF16)
        fb = _expand_rows(fre_ref[...], fim_ref[...])
        fb_hi = fb.astype(BF16)
        fb_lo = (fb - fb_hi.astype(F32)).astype(BF16)
        c_hi, c_lo = ct_ref[:tw, :], clo_ref[...]
        krow = (lax.dot_general(fb_hi, c_hi, _NT, preferred_element_type=F32)
                + lax.dot_general(fb_lo, c_hi, _NT, preferred_element_type=F32)
                + lax.dot_general(fb_hi, c_lo, _NT, preferred_element_type=F32)).astype(BF16)
        for k in range(t_len):
            parts = [krow[:, :(t_len - k) * LANES]]
            if k:
                parts = [jnp.zeros((LANES, k * LANES), BF16)] + parts
            m_ref[k * LANES:(k + 1) * LANES, :] = jnp.concatenate(parts, axis=1)

    u = u_ref[...]
    ub = u.astype(BF16)
    sp = sp_ref[...].astype(BF16)
    for j in range(t_len // 2):
        lo, hi = 2 * j * LANES, (2 * j + 2) * LANES
        y = jnp.dot(ub[:, :hi], m_ref[:hi, lo:hi], preferred_element_type=F32)
        y = y + lax.dot_general(sp, ct_ref[LANES + lo:LANES + hi, :], _NT, preferred_element_type=F32)
        y = jax.nn.gelu(y + d_ref[:, lo:hi] * u[:, lo:hi], approximate=True)
        for t in (2 * j, 2 * j + 1):
            y_ref[pl.ds(t, tr, stride=t_len), :] = y[:, t * LANES - lo:(t + 1) * LANES - lo]


def _s5_post_kernel(y_ref, z_ref, h_ref, wg_ref, bg_ref, wo_ref, o_ref):
    y = jnp.concatenate([y_ref[s] for s in range(y_ref.shape[0])], axis=-1)
    lin = jnp.dot(y.astype(BF16), wg_ref[...], preferred_element_type=F32) + bg_ref[...]
    gated = y * jax.nn.sigmoid(lin) * jax.nn.silu(z_ref[...].astype(F32))
    o_ref[...] = h_ref[...] + jnp.dot(gated.astype(BF16), wo_ref[...], preferred_element_type=F32)


def _s5_derived(a_re, a_im, log_dt, b_re, b_im, c_re, c_im, d_skip):
    t_len = SSM_CHUNK
    g_cnt, p_cnt = a_re.shape
    n_slab = g_cnt // SLAB_GROUPS
    dt = jnp.exp(log_dt)[:, None]

    def dup(x):
        return jnp.concatenate([x, x], axis=-1)

    def powers(exps, lam_re, lam_im):
        j = exps.astype(F32)[:, None, None]
        mag_j = jnp.exp(j * lam_re)
        return mag_j * jnp.cos(j * lam_im), mag_j * jnp.sin(j * lam_im)

    lr, li = dup(a_re), dup(a_im)
    lam_re, lam_im = lr * dt, li * dt
    mag = jnp.exp(lam_re)
    ab_re, ab_im = mag * jnp.cos(lam_im), mag * jnp.sin(lam_im)
    den = lr * lr + li * li
    nr, ni = ab_re - 1.0, ab_im
    f_re = (nr * lr + ni * li) / den
    f_im = (ni * lr - nr * li) / den
    pr, pi = powers(t_len - 1 - jnp.arange(t_len), lam_re, lam_im)
    w_re = (pr * f_re - pi * f_im)[:, :, None, :]
    w_im = (pr * f_im + pi * f_re)[:, :, None, :]
    bt_re, bt_im = dup(b_re.transpose(0, 2, 1)), dup(b_im.transpose(0, 2, 1))
    wz_re = w_re * bt_re - w_im * bt_im
    wz_im = w_re * bt_im + w_im * bt_re
    qr, qi = powers(jnp.arange(t_len + 1), lam_re, lam_im)
    qr, qi = qr[:, :, None, :], qi[:, :, None, :]
    ct_re, ct_im = dup(c_re), dup(c_im)
    cv_re = ct_re * qr - ct_im * qi
    cv_im = -(ct_re * qi + ct_im * qr)

    def rows(x):
        return x.reshape(x.shape[0], g_cnt * SSM_GROUP, 2 * p_cnt)

    ap_re, ap_im = powers(t_len * jnp.arange(1, SUBLANES + 1), a_re * dt, a_im * dt)
    a_pow = jnp.stack([ap_re.reshape(SUBLANES, n_slab, -1), ap_im.reshape(SUBLANES, n_slab, -1)], axis=2)
    a_pow = a_pow.reshape(SUBLANES, n_slab * SLAB_STATE)
    d_til = jnp.tile(d_skip.reshape(n_slab, 1, LANES), (1, 1, t_len))
    return rows(wz_re), rows(wz_im), rows(cv_re), rows(cv_im), a_pow, d_til


def _s5_layer(h, g, w_in, a_re, a_im, log_dt, b_re, b_im, c_re, c_im, d_skip, w_glu, b_glu, w_out):
    seq, d = h.shape
    n_slab = d // LANES
    t_len = SSM_CHUNK
    n_chunk = seq // t_len
    tw = t_len * LANES
    sw = SLAB_STATE
    wz_re, wz_im, cv_re, cv_im, a_pow, d_til = _s5_derived(
        a_re, a_im, log_dt, b_re, b_im, c_re, c_im, d_skip)

    tm = ROW_TILE
    uc, z = pl.pallas_call(
        _s5_in_kernel,
        grid=(seq // tm,),
        in_specs=[pl.BlockSpec((tm, d), lambda i: (i, 0)),
                  pl.BlockSpec((1, d), lambda i: (0, 0)),
                  pl.BlockSpec((d, 2 * d), lambda i: (0, 0))],
        out_specs=[pl.BlockSpec((n_slab, tm // t_len, tw), lambda i: (0, i, 0)),
                   pl.BlockSpec((tm, d), lambda i: (i, 0))],
        out_shape=[jax.ShapeDtypeStruct((n_slab, n_chunk, tw), F32),
                   jax.ShapeDtypeStruct((seq, d), BF16)],
        scratch_shapes=[pltpu.VMEM((n_slab, tm, LANES), F32)],
        compiler_params=_cparams("parallel"),
        name="s5_in",
    )(h, g.reshape(1, d), w_in.astype(BF16))

    tr = CHUNK_TILE
    steps = lambda s, i: (0, s, 0)
    zst = pl.pallas_call(
        _ssm_state_kernel,
        grid=(n_slab, n_chunk // tr),
        in_specs=[pl.BlockSpec((None, tr, tw), lambda s, i: (s, i, 0)),
                  pl.BlockSpec((t_len, LANES, LANES), steps),
                  pl.BlockSpec((t_len, LANES, LANES), steps)],
        out_specs=pl.BlockSpec((tr, sw), lambda s, i: (i, s)),
        out_shape=jax.ShapeDtypeStruct((n_chunk, n_slab * sw), F32),
        scratch_shapes=[pltpu.VMEM((tw, sw), BF16)],
        compiler_params=_cparams("parallel", "arbitrary"),
        name="ssm_state",
    )(uc, wz_re, wz_im)

    tc = SCAN_TILE
    sprev = pl.pallas_call(
        _ssm_scan_kernel,
        grid=(n_chunk // tc,),
        in_specs=[pl.BlockSpec((tc, n_slab * sw), lambda i: (i, 0)),
                  pl.BlockSpec((SUBLANES, n_slab * sw), lambda i: (0, 0))],
        out_specs=pl.BlockSpec((tc, n_slab * sw), lambda i: (i, 0)),
        out_shape=jax.ShapeDtypeStruct((n_chunk, n_slab * sw), F32),
        scratch_shapes=[pltpu.VMEM((SUBLANES, n_slab * sw), F32)],
        compiler_params=_cparams("arbitrary"),
        name="ssm_scan",
    )(zst, a_pow)

    last_step = lambda s, i: (t_len - 1, s, 0)
    yg = pl.pallas_call(
        _ssm_out_kernel,
        grid=(n_slab, n_chunk // tr),
        in_specs=[pl.BlockSpec((None, tr, tw), lambda s, i: (s, i, 0)),
                  pl.BlockSpec((tr, sw), lambda s, i: (i, s)),
                  pl.BlockSpec((t_len + 1, LANES, LANES), steps),
                  pl.BlockSpec((t_len + 1, LANES, LANES), steps),
                  pl.BlockSpec((None, LANES, LANES), last_step),
                  pl.BlockSpec((None, LANES, LANES), last_step),
                  pl.BlockSpec((None, 1, tw), lambda s, i: (s, 0, 0))],
        out_specs=pl.BlockSpec((None, tr * t_len, LANES), lambda s, i: (s, i, 0)),
        out_shape=jax.ShapeDtypeStruct((n_slab, seq, LANES), F32),
        scratch_shapes=[pltpu.VMEM((tw, tw), BF16),
                        pltpu.VMEM((tw + LANES, sw), BF16),
                        pltpu.VMEM((tw, sw), BF16)],
        compiler_params=_cparams("parallel", "arbitrary"),
        name="ssm_out",
    )(uc, sprev, cv_re, cv_im, wz_re, wz_im, d_til)

    return pl.pallas_call(
        _s5_post_kernel,
        grid=(seq // tm,),
        in_specs=[pl.BlockSpec((n_slab, tm, LANES), lambda i: (0, i, 0)),
                  pl.BlockSpec((tm, d), lambda i: (i, 0)),
                  pl.BlockSpec((tm, d), lambda i: (i, 0)),
                  pl.BlockSpec((d, d), lambda i: (0, 0)),
                  pl.BlockSpec((1, d), lambda i: (0, 0)),
                  pl.BlockSpec((d, d), lambda i: (0, 0))],
        out_specs=pl.BlockSpec((tm, d), lambda i: (i, 0)),
        out_shape=jax.ShapeDtypeStruct((seq, d), F32),
        compiler_params=_cparams("parallel"),
        name="s5_post",
    )(yg, z, h, w_glu.astype(BF16), b_glu.reshape(1, d), w_out.astype(BF16))


def _attn_in_kernel(h_ref, g_ref, wqkz_ref, wvt_ref, qg_ref, kg_ref, cos_ref, sa_ref, sb_ref,
                    q_ref, ka_ref, km_ref, vt_ref, z_ref):
    tm, d = h_ref.shape
    n_head = d // HEAD_DIM
    n_blk = tm // MOBA_BLOCK
    i = pl.program_id(0)
    hn = _rms(h_ref[...], g_ref[...]).astype(BF16)
    cosf, sa, sb = cos_ref[...], sa_ref[...], sb_ref[...]

    def proj(col, width):
        return jnp.dot(hn, wqkz_ref[:, col:col + width], preferred_element_type=F32)

    def norm_rope(t, gain):
        t = _rms(t, gain)
        up = pltpu.roll(t, HEAD_DIM - ROT_DIM // 2, 1)
        dn = pltpu.roll(t, ROT_DIM // 2, 1)
        return t * cosf + up * sa + dn * sb

    scale = HEAD_DIM ** -0.5 * LOG2_E
    pairw = 2 * HEAD_DIM
    for hp in range(n_head // 2):
        qp = proj(hp * pairw, pairw)
        kp = proj(d + hp * pairw, pairw)
        for sub in range(2):
            hd = 2 * hp + sub
            qh = norm_rope(qp[:, sub * HEAD_DIM:(sub + 1) * HEAD_DIM], qg_ref[...])
            q_ref[hd] = qh * scale
            kh = norm_rope(kp[:, sub * HEAD_DIM:(sub + 1) * HEAD_DIM], kg_ref[...])
            ka_ref[hd, :, :HEAD_DIM] = kh.astype(BF16)
            for b in range(n_blk):
                km_ref[b, hd:hd + 1, :] = jnp.mean(
                    kh[b * MOBA_BLOCK:(b + 1) * MOBA_BLOCK], axis=0, keepdims=True)
    vt = lax.dot_general(wvt_ref[...], hn, _NT, preferred_element_type=F32)
    lane = lax.broadcasted_iota(jnp.int32, (MOBA_BLOCK, LANES), 1)
    pad_row = lax.broadcasted_iota(jnp.int32, (V_ROWS - HEAD_DIM, MOBA_BLOCK), 0)
    ones_row = jnp.where(pad_row == 0, 1.0, 0.0).astype(BF16)
    for b in range(n_blk):
        onehot = jnp.where(lane == i * n_blk + b, NEG_BIG, 0.0).astype(BF16)
        for hd in range(n_head):
            ka_ref[hd, b * MOBA_BLOCK:(b + 1) * MOBA_BLOCK, HEAD_DIM:] = onehot
        for hd in range(n_head):
            vt_ref[b, hd * V_ROWS:hd * V_ROWS + HEAD_DIM, :] = vt[
                hd * HEAD_DIM:(hd + 1) * HEAD_DIM, b * MOBA_BLOCK:(b + 1) * MOBA_BLOCK].astype(BF16)
            vt_ref[b, hd * V_ROWS + HEAD_DIM:(hd + 1) * V_ROWS, :] = ones_row
    z_ref[...] = proj(2 * d, d).astype(BF16)


def _moba_select_kernel(q_ref, km_ref, qa_ref):
    ts, hd = q_ref.shape
    n_blk = km_ref.shape[0]
    q = q_ref[...]
    gates = lax.dot_general(km_ref[...], q, _NT, precision=lax.Precision.HIGHEST,
                            preferred_element_type=F32)
    nidx = lax.broadcasted_iota(jnp.int32, gates.shape, 0)
    qpos = pl.program_id(1) * ts + lax.broadcasted_iota(jnp.int32, gates.shape, 1)
    own = qpos >> _log2(MOBA_BLOCK)
    valid = nidx < own
    g = jnp.where(valid, gates, -jnp.inf)
    picked = jnp.zeros(gates.shape, jnp.bool_)
    for _ in range(MOBA_TOPK):
        mx = jnp.max(g, axis=0, keepdims=True)
        first = jnp.min(jnp.where(g == mx, nidx, n_blk), axis=0, keepdims=True)
        hit = nidx == first
        picked = jnp.logical_or(picked, hit)
        g = jnp.where(hit, -jnp.inf, g)
    visible = jnp.logical_or(jnp.logical_and(valid, picked), nidx == own)
    qa_ref[:hd, :] = q.T.astype(BF16)
    qa_ref[hd:hd + n_blk, :] = jnp.where(visible, 0.0, 1.0).astype(BF16)
    qa_ref[hd + n_blk:, :] = jnp.zeros((qa_ref.shape[0] - hd - n_blk, ts), BF16)


def _moba_kernel(qa_ref, k_ref, vt_ref, o_ref, sa_ref, sb_ref, xa_ref, xb_ref, m_ref, acc_ref):
    step = pl.program_id(1)
    tq = qa_ref.shape[1]
    gk = KV_GROUP * MOBA_BLOCK
    assert KV_GROUP % (tq // MOBA_BLOCK) == 0

    m_ref[...] = jnp.full(m_ref.shape, -jnp.inf, F32)
    acc_ref[...] = jnp.zeros_like(acc_ref)

    def score(gi, dst_ref, max_ref):
        kt = k_ref[pl.ds(pl.multiple_of(gi * gk, gk), gk), :]
        s = jnp.dot(kt, qa_ref[...], preferred_element_type=F32)
        dst_ref[...] = s
        max_ref[...] = jnp.max(s, axis=0, keepdims=True)

    def rows(b):
        return slice(b * MOBA_BLOCK, (b + 1) * MOBA_BLOCK)

    def absorb(src_ref, max_ref, gi, causal):
        if causal and tq == gk:
            tri = (lax.broadcasted_iota(jnp.int32, (MOBA_BLOCK, MOBA_BLOCK), 0)
                   <= lax.broadcasted_iota(jnp.int32, (MOBA_BLOCK, MOBA_BLOCK), 1))
            for b in range(KV_GROUP):
                src_ref[rows(b), rows(b)] = jnp.where(tri, src_ref[rows(b), rows(b)], -jnp.inf)
        elif causal:
            kpos = gi * gk + lax.broadcasted_iota(jnp.int32, src_ref.shape, 0)
            qpos = step * tq + lax.broadcasted_iota(jnp.int32, src_ref.shape, 1)
            src_ref[...] = jnp.where(kpos <= qpos, src_ref[...], -jnp.inf)
        m_old = m_ref[...]
        if causal:
            m_new = m_old
            for b in range(KV_GROUP):
                m_new = jnp.maximum(m_new, jnp.max(src_ref[rows(b), :], axis=0, keepdims=True))
        else:
            m_new = jnp.maximum(m_old, max_ref[...])
        alpha = jnp.exp2(m_old - m_new)
        pv = alpha * acc_ref[...]
        for b in range(KV_GROUP):
            p = jnp.exp2(src_ref[rows(b), :] - m_new)
            pv = pv + jnp.dot(vt_ref[gi * KV_GROUP + b], p.astype(BF16), preferred_element_type=F32)
        acc_ref[...] = pv
        m_ref[...] = m_new

    n_full = (step * (tq // MOBA_BLOCK)) // KV_GROUP
    score(0, sa_ref, xa_ref)

    def pair(i, carry):
        score(2 * i + 1, sb_ref, xb_ref)
        absorb(sa_ref, xa_ref, 2 * i, False)
        score(2 * i + 2, sa_ref, xa_ref)
        absorb(sb_ref, xb_ref, 2 * i + 1, False)
        return carry

    lax.fori_loop(0, n_full // 2, pair, 0)

    @pl.when(n_full % 2 == 1)
    def _():
        score(n_full, sb_ref, xb_ref)
        absorb(sa_ref, xa_ref, n_full - 1, False)
        absorb(sb_ref, xb_ref, n_full, True)

    @pl.when(n_full % 2 == 0)
    def _():
        absorb(sa_ref, xa_ref, n_full, True)

    o_ref[...] = (acc_ref[:HEAD_DIM, :] / acc_ref[HEAD_DIM:HEAD_DIM + 1, :]).T.astype(o_ref.dtype)


def _attn_post_kernel(o_ref, z_ref, h_ref, wo_ref, out_ref):
    out = h_ref[...]
    for c in range(o_ref.shape[1] // MXU_COLS):
        cols = slice(c * MXU_COLS, (c + 1) * MXU_COLS)
        gated = o_ref[:, cols].astype(F32) * jax.nn.silu(z_ref[:, cols].astype(F32))
        out = out + jnp.dot(gated.astype(BF16), wo_ref[cols, :], preferred_element_type=F32)
    out_ref[...] = out


def _rope_tables(seq):
    half = ROT_DIM // 2
    inv_freq = ROPE_THETA ** (-(np.arange(half, dtype=np.float64) * 2.0) / ROT_DIM)
    ang = np.arange(seq, dtype=np.float64)[:, None] * inv_freq[None, :]
    cos, sin = np.cos(ang), np.sin(ang)
    pad = HEAD_DIM - ROT_DIM
    zeros = np.zeros((seq, half))
    cosf = np.concatenate([cos, cos, np.ones((seq, pad))], axis=1)
    sa = np.concatenate([-sin, zeros, np.zeros((seq, pad))], axis=1)
    sb = np.concatenate([zeros, sin, np.zeros((seq, pad))], axis=1)
    return tuple(jnp.asarray(t, F32) for t in (cosf, sa, sb))


def _moba_layer(h, g, w_in, q_gain, k_gain, w_out, rope):
    seq, d = h.shape
    n_head = d // HEAD_DIM
    n_blk = seq // MOBA_BLOCK
    assert n_blk <= LANES and n_blk % KV_GROUP == 0 and seq % MOBA_BLOCK == 0
    cosf, sa, sb = rope
    w_qkz = jnp.concatenate([w_in[:, :2 * d], w_in[:, 3 * d:]], axis=1).astype(BF16)
    w_vt = w_in[:, 2 * d:3 * d].astype(BF16).T

    tm = ROW_TILE
    bpt = tm // MOBA_BLOCK
    row = lambda i: (i, 0)
    const = lambda i: (0, 0)
    q, kaug, kmean, vt, z = pl.pallas_call(
        _attn_in_kernel,
        grid=(seq // tm,),
        in_specs=[pl.BlockSpec((tm, d), row),
                  pl.BlockSpec((1, d), const),
                  pl.BlockSpec((d, 3 * d), const),
                  pl.BlockSpec((d, d), const),
                  pl.BlockSpec((1, HEAD_DIM), const),
                  pl.BlockSpec((1, HEAD_DIM), const),
                  pl.BlockSpec((tm, HEAD_DIM), row),
                  pl.BlockSpec((tm, HEAD_DIM), row),
                  pl.BlockSpec((tm, HEAD_DIM), row)],
        out_specs=[pl.BlockSpec((n_head, tm, HEAD_DIM), lambda i: (0, i, 0)),
                   pl.BlockSpec((n_head, tm, 2 * HEAD_DIM), lambda i: (0, i, 0)),
                   pl.BlockSpec((bpt, n_head, HEAD_DIM), lambda i: (i, 0, 0)),
                   pl.BlockSpec((bpt, n_head * V_ROWS, MOBA_BLOCK), lambda i: (i, 0, 0)),
                   pl.BlockSpec((tm, d), row)],
        out_shape=[jax.ShapeDtypeStruct((n_head, seq, HEAD_DIM), F32),
                   jax.ShapeDtypeStruct((n_head, seq, 2 * HEAD_DIM), BF16),
                   jax.ShapeDtypeStruct((n_blk, n_head, HEAD_DIM), F32),
                   jax.ShapeDtypeStruct((n_blk, n_head * V_ROWS, MOBA_BLOCK), BF16),
                   jax.ShapeDtypeStruct((seq, d), BF16)],
        compiler_params=_cparams("parallel"),
        name="attn_in",
    )(h, g.reshape(1, d), w_qkz, w_vt, q_gain.reshape(1, HEAD_DIM), k_gain.reshape(1, HEAD_DIM),
      cosf, sa, sb)

    qa = pl.pallas_call(
        _moba_select_kernel,
        grid=(n_head, seq // SEL_TILE),
        in_specs=[pl.BlockSpec((None, SEL_TILE, HEAD_DIM), lambda hd, i: (hd, i, 0)),
                  pl.BlockSpec((None, n_blk, HEAD_DIM), lambda hd, i: (hd, 0, 0))],
        out_specs=pl.BlockSpec((None, 2 * HEAD_DIM, SEL_TILE), lambda hd, i: (hd, 0, i)),
        out_shape=jax.ShapeDtypeStruct((n_head, 2 * HEAD_DIM, seq), BF16),
        compiler_params=_cparams("parallel", "parallel"),
        name="moba_select",
    )(q, kmean.transpose(1, 0, 2))

    o = pl.pallas_call(
        _moba_kernel,
        grid=(n_head, seq // Q_TILE),
        in_specs=[pl.BlockSpec((None, 2 * HEAD_DIM, Q_TILE), lambda hd, i: (hd, 0, i)),
                  pl.BlockSpec((None, seq, 2 * HEAD_DIM), lambda hd, i: (hd, 0, 0)),
                  pl.BlockSpec((n_blk, V_ROWS, MOBA_BLOCK), lambda hd, i: (0, hd, 0))],
        out_specs=pl.BlockSpec((Q_TILE, HEAD_DIM), lambda hd, i: (i, hd)),
        out_shape=jax.ShapeDtypeStruct((seq, d), BF16),
        scratch_shapes=[pltpu.VMEM((KV_GROUP * MOBA_BLOCK, Q_TILE), F32),
                        pltpu.VMEM((KV_GROUP * MOBA_BLOCK, Q_TILE), F32),
                        pltpu.VMEM((1, Q_TILE), F32),
                        pltpu.VMEM((1, Q_TILE), F32),
                        pltpu.VMEM((1, Q_TILE), F32),
                        pltpu.VMEM((V_ROWS, Q_TILE), F32)],
        compiler_params=_cparams("parallel", "arbitrary"),
        name="moba_attn",
    )(qa, kaug, vt)

    return pl.pallas_call(
        _attn_post_kernel,
        grid=(seq // tm,),
        in_specs=[pl.BlockSpec((tm, d), row),
                  pl.BlockSpec((tm, d), row),
                  pl.BlockSpec((tm, d), row),
                  pl.BlockSpec((d, d), const)],
        out_specs=pl.BlockSpec((tm, d), row),
        out_shape=jax.ShapeDtypeStruct((seq, d), F32),
        compiler_params=_cparams("parallel"),
        name="attn_post",
    )(o, z, h, w_out.astype(BF16))


def kernel(x, norm_g, ssm_w_in, ssm_a_re, ssm_a_im, ssm_log_dt, ssm_b_re, ssm_b_im, ssm_c_re, ssm_c_im, ssm_d, ssm_w_glu, ssm_b_glu, ssm_w_out, attn_w_in, attn_q_gain, attn_k_gain, attn_w_out):
    bsz, seq, d = x.shape
    depth = norm_g.shape[0]
    rope = _rope_tables(seq)
    outs = []
    for b in range(bsz):
        h = x[b]
        for i in range(depth):
            j = i // 2
            if i % 2 == 0:
                h = _s5_layer(h, norm_g[i], ssm_w_in[j], ssm_a_re[j], ssm_a_im[j], ssm_log_dt[j],
                              ssm_b_re[j], ssm_b_im[j], ssm_c_re[j], ssm_c_im[j], ssm_d[j],
                              ssm_w_glu[j], ssm_b_glu[j], ssm_w_out[j])
            else:
                h = _moba_layer(h, norm_g[i], attn_w_in[j], attn_q_gain[j], attn_k_gain[j],
                                attn_w_out[j], rope)
        outs.append(h)
    return jnp.stack(outs)
```

```python
import jax
import jax.numpy as jnp
import numpy as np
from jax import lax
from jax.experimental import pallas as pl
from jax.experimental.pallas import tpu as pltpu

F32 = jnp.float32
BF16 = jnp.bfloat16

NORM_EPS = 1e-6
SSM_GROUP = 16
SSM_STATE = 64
HEAD_DIM = 128
ROT_DIM = HEAD_DIM // 4
ROPE_THETA = 500000.0
MOBA_BLOCK = 256
MOBA_TOPK = 3

LANES = 128
MXU_COLS = 256
SUBLANES = 8
SLAB_GROUPS = LANES // SSM_GROUP
SLAB_STATE = 2 * SLAB_GROUPS * SSM_STATE
SSM_CHUNK = 16
KV_GROUP = 4
NEG_BIG = -(2.0 ** 60)
LOG2_E = 1.4426950408889634
Q_TILE = 1024
V_ROWS = HEAD_DIM + 16
SEL_TILE = 2048
ROW_TILE = 512
POST_TILE = 1024
CHUNK_TILE = 512
SCAN_TILE = 256
VMEM_LIMIT = 56 * 1024 * 1024

_NT = (((1,), (1,)), ((), ()))


def _cparams(*sem):
    return pltpu.CompilerParams(dimension_semantics=sem, vmem_limit_bytes=VMEM_LIMIT)


def _rms(x, g):
    ms = jnp.mean(x * x, axis=-1, keepdims=True)
    return x * lax.rsqrt(ms + NORM_EPS) * g


def _log2(n):
    assert n & (n - 1) == 0
    return n.bit_length() - 1


def _cast_once(w_ref, wb_ref):
    @pl.when(pl.program_id(0) == 0)
    def _():
        wb_ref[...] = w_ref[...].astype(BF16)


def _s5_in_kernel(h_ref, g_ref, w_ref, u_ref, z_ref, us_ref, wb_ref):
    tm, d = h_ref.shape
    n_row = tm // SSM_CHUNK
    _cast_once(w_ref, wb_ref)
    hn = _rms(h_ref[...], g_ref[...]).astype(BF16)
    proj = jnp.dot(hn, wb_ref[...], preferred_element_type=F32)
    z_ref[...] = proj[:, d:].astype(BF16)
    for s in range(d // LANES):
        us_ref[s] = proj[:, s * LANES:(s + 1) * LANES]
        for t in range(SSM_CHUNK):
            u_ref[s, :, t * LANES:(t + 1) * LANES] = us_ref[s, pl.ds(t, n_row, stride=SSM_CHUNK), :]


def _expand_rows(re2, im2):
    r = lax.broadcasted_iota(jnp.int32, re2.shape, 0)
    lane = lax.broadcasted_iota(jnp.int32, re2.shape, 1)
    row_group = r >> _log2(SSM_GROUP)
    lane_half = lane >> _log2(SSM_STATE)
    cols_re, cols_im = [], []
    for j in range(SLAB_STATE // 2 // LANES):
        msk = row_group == 2 * j + lane_half
        cols_re.append(jnp.where(msk, re2, 0.0))
        cols_im.append(jnp.where(msk, im2, 0.0))
    return jnp.concatenate(cols_re + cols_im, axis=1)


def _ssm_state_kernel(u_ref, re2_ref, im2_ref, z_ref, wz_ref):
    @pl.when(pl.program_id(1) == 0)
    def _():
        for k in range(re2_ref.shape[0]):
            wz_ref[k * LANES:(k + 1) * LANES, :] = _expand_rows(re2_ref[k], im2_ref[k]).astype(BF16)

    z_ref[...] = jnp.dot(u_ref[...].astype(BF16), wz_ref[...], preferred_element_type=F32)


def _ssm_scan_kernel(z_ref, ap_ref, o_ref, last_ref):
    half = SLAB_STATE // 2
    n_slab = z_ref.shape[1] // SLAB_STATE

    @pl.when(pl.program_id(0) == 0)
    def _():
        last_ref[...] = jnp.zeros_like(last_ref)

    rows = lax.broadcasted_iota(jnp.int32, (SUBLANES, half), 0)

    def shift_down(x, k, fill):
        return jnp.where(rows >= k, pltpu.roll(x, k, 0), fill)

    def body(step, _):
        r0 = pl.multiple_of(step * SUBLANES, SUBLANES)
        for s in range(n_slab):
            re_sl = slice(s * SLAB_STATE, s * SLAB_STATE + half)
            im_sl = slice(s * SLAB_STATE + half, (s + 1) * SLAB_STATE)
            xr = z_ref[pl.ds(r0, SUBLANES), re_sl]
            xi = z_ref[pl.ds(r0, SUBLANES), im_sl]
            for k in (1, 2, 4):
                ar = ap_ref[k - 1:k, re_sl]
                ai = ap_ref[k - 1:k, im_sl]
                sr, si = shift_down(xr, k, 0.0), shift_down(xi, k, 0.0)
                xr, xi = xr + ar * sr - ai * si, xi + ar * si + ai * sr
            cr = jnp.broadcast_to(last_ref[SUBLANES - 1:SUBLANES, re_sl], (SUBLANES, half))
            ci = jnp.broadcast_to(last_ref[SUBLANES - 1:SUBLANES, im_sl], (SUBLANES, half))
            apr, api = ap_ref[:, re_sl], ap_ref[:, im_sl]
            xr, xi = xr + apr * cr - api * ci, xi + apr * ci + api * cr
            o_ref[pl.ds(r0, SUBLANES), re_sl] = shift_down(xr, 1, cr)
            o_ref[pl.ds(r0, SUBLANES), im_sl] = shift_down(xi, 1, ci)
            last_ref[:, re_sl] = xr
            last_ref[:, im_sl] = xi
        return 0

    lax.fori_loop(0, z_ref.shape[0] // SUBLANES, body, 0)


def _ssm_out_kernel(u_ref, sp_ref, cre_ref, cim_ref, fre_ref, fim_ref, d_ref, y_ref,
                    m_ref, ct_ref, clo_ref):
    t_len = SSM_CHUNK
    tr = u_ref.shape[0]
    tw = t_len * LANES

    @pl.when(pl.program_id(1) == 0)
    def _():
        for t in range(t_len + 1):
            x = _expand_rows(cre_ref[t], cim_ref[t])
            hi = x.astype(BF16)
            ct_ref[t * LANES:(t + 1) * LANES, :] = hi
            if t < t_len:
                clo_ref[t * LANES:(t + 1) * LANES, :] = (x - hi.astype(F32)).astype(BF16)
        fb = _expand_rows(fre_ref[...], fim_ref[...])
        fb_hi = fb.astype(BF16)
        fb_lo = (fb - fb_hi.astype(F32)).astype(BF16)
        c_hi, c_lo = ct_ref[:tw, :], clo_ref[...]
        krow = (lax.dot_general(fb_hi, c_hi, _NT, preferred_element_type=F32)
                + lax.dot_general(fb_lo, c_hi, _NT, preferred_element_type=F32)
                + lax.dot_general(fb_hi, c_lo, _NT, preferred_element_type=F32)).astype(BF16)
        for k in range(t_len):
            parts = [krow[:, :(t_len - k) * LANES]]
            if k:
                parts = [jnp.zeros((LANES, k * LANES), BF16)] + parts
            m_ref[k * LANES:(k + 1) * LANES, :] = jnp.concatenate(parts, axis=1)

    u = u_ref[...]
    ub = u.astype(BF16)
    sp = sp_ref[...].astype(BF16)
    for j in range(t_len // 2):
        lo, hi = 2 * j * LANES, (2 * j + 2) * LANES
        y = jnp.dot(ub[:, :hi], m_ref[:hi, lo:hi], preferred_element_type=F32)
        y = y + lax.dot_general(sp, ct_ref[LANES + lo:LANES + hi, :], _NT, preferred_element_type=F32)
        y = jax.nn.gelu(y + d_ref[:, lo:hi] * u[:, lo:hi], approximate=True)
        for t in (2 * j, 2 * j + 1):
            y_ref[pl.ds(t, tr, stride=t_len), :] = y[:, t * LANES - lo:(t + 1) * LANES - lo]


def _s5_post_kernel(y_ref, z_ref, h_ref, wg_ref, bg_ref, wo_ref, o_ref, wgb_ref, wob_ref):
    _cast_once(wg_ref, wgb_ref)
    _cast_once(wo_ref, wob_ref)
    y = jnp.concatenate([y_ref[s] for s in range(y_ref.shape[0])], axis=-1)
    lin = jnp.dot(y.astype(BF16), wgb_ref[...], preferred_element_type=F32) + bg_ref[...]
    gated = y * jax.nn.sigmoid(lin) * jax.nn.silu(z_ref[...].astype(F32))
    o_ref[...] = h_ref[...] + jnp.dot(gated.astype(BF16), wob_ref[...], preferred_element_type=F32)


def _s5_derived(a_re, a_im, log_dt, b_re, b_im, c_re, c_im, d_skip):
    t_len = SSM_CHUNK
    g_cnt, p_cnt = a_re.shape
    n_slab = g_cnt // SLAB_GROUPS
    dt = jnp.exp(log_dt)[:, None]

    def dup(x):
        return jnp.concatenate([x, x], axis=-1)

    def powers(exps, lam_re, lam_im):
        j = exps.astype(F32)[:, None, None]
        mag_j = jnp.exp(j * lam_re)
        return mag_j * jnp.cos(j * lam_im), mag_j * jnp.sin(j * lam_im)

    lr, li = dup(a_re), dup(a_im)
    lam_re, lam_im = lr * dt, li * dt
    mag = jnp.exp(lam_re)
    ab_re, ab_im = mag * jnp.cos(lam_im), mag * jnp.sin(lam_im)
    den = lr * lr + li * li
    nr, ni = ab_re - 1.0, ab_im
    f_re = (nr * lr + ni * li) / den
    f_im = (ni * lr - nr * li) / den
    pr, pi = powers(t_len - 1 - jnp.arange(t_len), lam_re, lam_im)
    w_re = (pr * f_re - pi * f_im)[:, :, None, :]
    w_im = (pr * f_im + pi * f_re)[:, :, None, :]
    bt_re, bt_im = dup(b_re.transpose(0, 2, 1)), dup(b_im.transpose(0, 2, 1))
    wz_re = w_re * bt_re - w_im * bt_im
    wz_im = w_re * bt_im + w_im * bt_re
    qr, qi = powers(jnp.arange(t_len + 1), lam_re, lam_im)
    qr, qi = qr[:, :, None, :], qi[:, :, None, :]
    ct_re, ct_im = dup(c_re), dup(c_im)
    cv_re = ct_re * qr - ct_im * qi
    cv_im = -(ct_re * qi + ct_im * qr)

    def rows(x):
        return x.reshape(x.shape[0], g_cnt * SSM_GROUP, 2 * p_cnt)

    ap_re, ap_im = powers(t_len * jnp.arange(1, SUBLANES + 1), a_re * dt, a_im * dt)
    a_pow = jnp.stack([ap_re.reshape(SUBLANES, n_slab, -1), ap_im.reshape(SUBLANES, n_slab, -1)], axis=2)
    a_pow = a_pow.reshape(SUBLANES, n_slab * SLAB_STATE)
    d_til = jnp.tile(d_skip.reshape(n_slab, 1, LANES), (1, 1, t_len))
    return rows(wz_re), rows(wz_im), rows(cv_re), rows(cv_im), a_pow, d_til


def _s5_layer(h, g, w_in, a_re, a_im, log_dt, b_re, b_im, c_re, c_im, d_skip, w_glu, b_glu, w_out):
    seq, d = h.shape
    n_slab = d // LANES
    t_len = SSM_CHUNK
    n_chunk = seq // t_len
    tw = t_len * LANES
    sw = SLAB_STATE
    wz_re, wz_im, cv_re, cv_im, a_pow, d_til = _s5_derived(
        a_re, a_im, log_dt, b_re, b_im, c_re, c_im, d_skip)

    tm = ROW_TILE
    uc, z = pl.pallas_call(
        _s5_in_kernel,
        grid=(seq // tm,),
        in_specs=[pl.BlockSpec((tm, d), lambda i: (i, 0)),
                  pl.BlockSpec((1, d), lambda i: (0, 0)),
                  pl.BlockSpec((d, 2 * d), lambda i: (0, 0))],
        out_specs=[pl.BlockSpec((n_slab, tm // t_len, tw), lambda i: (0, i, 0)),
                   pl.BlockSpec((tm, d), lambda i: (i, 0))],
        out_shape=[jax.ShapeDtypeStruct((n_slab, n_chunk, tw), F32),
                   jax.ShapeDtypeStruct((seq, d), BF16)],
        scratch_shapes=[pltpu.VMEM((n_slab, tm, LANES), F32), pltpu.VMEM((d, 2 * d), BF16)],
        compiler_params=_cparams("arbitrary"),
        name="s5_in",
    )(h, g.reshape(1, d), w_in)

    tr = CHUNK_TILE
    steps = lambda s, i: (0, s, 0)
    zst = pl.pallas_call(
        _ssm_state_kernel,
        grid=(n_slab, n_chunk // tr),
        in_specs=[pl.BlockSpec((None, tr, tw), lambda s, i: (s, i, 0)),
                  pl.BlockSpec((t_len, LANES, LANES), steps),
                  pl.BlockSpec((t_len, LANES, LANES), steps)],
        out_specs=pl.BlockSpec((tr, sw), lambda s, i: (i, s)),
        out_shape=jax.ShapeDtypeStruct((n_chunk, n_slab * sw), F32),
        scratch_shapes=[pltpu.VMEM((tw, sw), BF16)],
        compiler_params=_cparams("parallel", "arbitrary"),
        name="ssm_state",
    )(uc, wz_re, wz_im)

    tc = SCAN_TILE
    sprev = pl.pallas_call(
        _ssm_scan_kernel,
        grid=(n_chunk // tc,),
        in_specs=[pl.BlockSpec((tc, n_slab * sw), lambda i: (i, 0)),
                  pl.BlockSpec((SUBLANES, n_slab * sw), lambda i: (0, 0))],
        out_specs=pl.BlockSpec((tc, n_slab * sw), lambda i: (i, 0)),
        out_shape=jax.ShapeDtypeStruct((n_chunk, n_slab * sw), F32),
        scratch_shapes=[pltpu.VMEM((SUBLANES, n_slab * sw), F32)],
        compiler_params=_cparams("arbitrary"),
        name="ssm_scan",
    )(zst, a_pow)

    last_step = lambda s, i: (t_len - 1, s, 0)
    yg = pl.pallas_call(
        _ssm_out_kernel,
        grid=(n_slab, n_chunk // tr),
        in_specs=[pl.BlockSpec((None, tr, tw), lambda s, i: (s, i, 0)),
                  pl.BlockSpec((tr, sw), lambda s, i: (i, s)),
                  pl.BlockSpec((t_len + 1, LANES, LANES), steps),
                  pl.BlockSpec((t_len + 1, LANES, LANES), steps),
                  pl.BlockSpec((None, LANES, LANES), last_step),
                  pl.BlockSpec((None, LANES, LANES), last_step),
                  pl.BlockSpec((None, 1, tw), lambda s, i: (s, 0, 0))],
        out_specs=pl.BlockSpec((None, tr * t_len, LANES), lambda s, i: (s, i, 0)),
        out_shape=jax.ShapeDtypeStruct((n_slab, seq, LANES), F32),
        scratch_shapes=[pltpu.VMEM((tw, tw), BF16),
                        pltpu.VMEM((tw + LANES, sw), BF16),
                        pltpu.VMEM((tw, sw), BF16)],
        compiler_params=_cparams("parallel", "arbitrary"),
        name="ssm_out",
    )(uc, sprev, cv_re, cv_im, wz_re, wz_im, d_til)

    tp = POST_TILE
    return pl.pallas_call(
        _s5_post_kernel,
        grid=(seq // tp,),
        in_specs=[pl.BlockSpec((n_slab, tp, LANES), lambda i: (0, i, 0)),
                  pl.BlockSpec((tp, d), lambda i: (i, 0)),
                  pl.BlockSpec((tp, d), lambda i: (i, 0)),
                  pl.BlockSpec((d, d), lambda i: (0, 0)),
                  pl.BlockSpec((1, d), lambda i: (0, 0)),
                  pl.BlockSpec((d, d), lambda i: (0, 0))],
        out_specs=pl.BlockSpec((tp, d), lambda i: (i, 0)),
        out_shape=jax.ShapeDtypeStruct((seq, d), F32),
        scratch_shapes=[pltpu.VMEM((d, d), BF16), pltpu.VMEM((d, d), BF16)],
        compiler_params=_cparams("arbitrary"),
        name="s5_post",
    )(yg, z, h, w_glu, b_glu.reshape(1, d), w_out)


def _attn_in_kernel(h_ref, g_ref, wqkz_ref, wvt_ref, qg_ref, kg_ref, cos_ref, sa_ref, sb_ref,
                    q_ref, ka_ref, km_ref, vt_ref, z_ref):
    tm, d = h_ref.shape
    n_head = d // HEAD_DIM
    n_blk = tm // MOBA_BLOCK
    i = pl.program_id(0)
    hn = _rms(h_ref[...], g_ref[...]).astype(BF16)
    cosf, sa, sb = cos_ref[...], sa_ref[...], sb_ref[...]

    def proj(col, width):
        return jnp.dot(hn, wqkz_ref[:, col:col + width], preferred_element_type=F32)

    def norm_rope(t, gain):
        t = _rms(t, gain)
        up = pltpu.roll(t, HEAD_DIM - ROT_DIM // 2, 1)
        dn = pltpu.roll(t, ROT_DIM // 2, 1)
        return t * cosf + up * sa + dn * sb

    scale = HEAD_DIM ** -0.5 * LOG2_E
    pairw = 2 * HEAD_DIM
    for hp in range(n_head // 2):
        qp = proj(hp * pairw, pairw)
        kp = proj(d + hp * pairw, pairw)
        for sub in range(2):
            hd = 2 * hp + sub
            qh = norm_rope(qp[:, sub * HEAD_DIM:(sub + 1) * HEAD_DIM], qg_ref[...])
            q_ref[hd] = qh * scale
            kh = norm_rope(kp[:, sub * HEAD_DIM:(sub + 1) * HEAD_DIM], kg_ref[...])
            ka_ref[hd, :, :HEAD_DIM] = kh.astype(BF16)
            for b in range(n_blk):
                km_ref[b, hd:hd + 1, :] = jnp.mean(
                    kh[b * MOBA_BLOCK:(b + 1) * MOBA_BLOCK], axis=0, keepdims=True)
    vt = lax.dot_general(wvt_ref[...], hn, _NT, preferred_element_type=F32)
    lane = lax.broadcasted_iota(jnp.int32, (MOBA_BLOCK, LANES), 1)
    pad_row = lax.broadcasted_iota(jnp.int32, (V_ROWS - HEAD_DIM, MOBA_BLOCK), 0)
    ones_row = jnp.where(pad_row == 0, 1.0, 0.0).astype(BF16)
    for b in range(n_blk):
        onehot = jnp.where(lane == i * n_blk + b, NEG_BIG, 0.0).astype(BF16)
        for hd in range(n_head):
            ka_ref[hd, b * MOBA_BLOCK:(b + 1) * MOBA_BLOCK, HEAD_DIM:] = onehot
        for hd in range(n_head):
            vt_ref[b, hd * V_ROWS:hd * V_ROWS + HEAD_DIM, :] = vt[
                hd * HEAD_DIM:(hd + 1) * HEAD_DIM, b * MOBA_BLOCK:(b + 1) * MOBA_BLOCK].astype(BF16)
            vt_ref[b, hd * V_ROWS + HEAD_DIM:(hd + 1) * V_ROWS, :] = ones_row
    z_ref[...] = proj(2 * d, d).astype(BF16)


def _moba_select_kernel(q_ref, km_ref, qa_ref):
    ts, hd = q_ref.shape
    n_blk = km_ref.shape[0]
    q = q_ref[...]
    gates = lax.dot_general(km_ref[...], q, _NT, precision=lax.Precision.HIGHEST,
                            preferred_element_type=F32)
    nidx = lax.broadcasted_iota(jnp.int32, gates.shape, 0)
    qpos = pl.program_id(1) * ts + lax.broadcasted_iota(jnp.int32, gates.shape, 1)
    own = qpos >> _log2(MOBA_BLOCK)
    valid = nidx < own
    g = jnp.where(valid, gates, -jnp.inf)
    picked = jnp.zeros(gates.shape, jnp.bool_)
    for _ in range(MOBA_TOPK):
        mx = jnp.max(g, axis=0, keepdims=True)
        first = jnp.min(jnp.where(g == mx, nidx, n_blk), axis=0, keepdims=True)
        hit = nidx == first
        picked = jnp.logical_or(picked, hit)
        g = jnp.where(hit, -jnp.inf, g)
    visible = jnp.logical_or(jnp.logical_and(valid, picked), nidx == own)
    qa_ref[:hd, :] = q.T.astype(BF16)
    qa_ref[hd:hd + n_blk, :] = jnp.where(visible, 0.0, 1.0).astype(BF16)
    qa_ref[hd + n_blk:, :] = jnp.zeros((qa_ref.shape[0] - hd - n_blk, ts), BF16)


def _moba_kernel(qa_ref, k_ref, vt_ref, o_ref, sa_ref, sb_ref, xa_ref, xb_ref, m_ref, acc_ref):
    step = pl.program_id(1)
    tq = qa_ref.shape[1]
    gk = KV_GROUP * MOBA_BLOCK
    assert KV_GROUP % (tq // MOBA_BLOCK) == 0

    m_ref[...] = jnp.full(m_ref.shape, -jnp.inf, F32)
    acc_ref[...] = jnp.zeros_like(acc_ref)

    def score(gi, dst_ref, max_ref):
        kt = k_ref[pl.ds(pl.multiple_of(gi * gk, gk), gk), :]
        s = jnp.dot(kt, qa_ref[...], preferred_element_type=F32)
        dst_ref[...] = s
        max_ref[...] = jnp.max(s, axis=0, keepdims=True)

    def rows(b):
        return slice(b * MOBA_BLOCK, (b + 1) * MOBA_BLOCK)

    def absorb(src_ref, max_ref, gi, causal):
        if causal and tq == gk:
            tri = (lax.broadcasted_iota(jnp.int32, (MOBA_BLOCK, MOBA_BLOCK), 0)
                   <= lax.broadcasted_iota(jnp.int32, (MOBA_BLOCK, MOBA_BLOCK), 1))
            for b in range(KV_GROUP):
                src_ref[rows(b), rows(b)] = jnp.where(tri, src_ref[rows(b), rows(b)], -jnp.inf)
        elif causal:
            kpos = gi * gk + lax.broadcasted_iota(jnp.int32, src_ref.shape, 0)
            qpos = step * tq + lax.broadcasted_iota(jnp.int32, src_ref.shape, 1)
            src_ref[...] = jnp.where(kpos <= qpos, src_ref[...], -jnp.inf)
        m_old = m_ref[...]
        if causal:
            m_new = m_old
            for b in range(KV_GROUP):
                m_new = jnp.maximum(m_new, jnp.max(src_ref[rows(b), :], axis=0, keepdims=True))
        else:
            m_new = jnp.maximum(m_old, max_ref[...])
        alpha = jnp.exp2(m_old - m_new)
        pv = alpha * acc_ref[...]
        for b in range(KV_GROUP):
            p = jnp.exp2(src_ref[rows(b), :] - m_new)
            pv = pv + jnp.dot(vt_ref[gi * KV_GROUP + b], p.astype(BF16), preferred_element_type=F32)
        acc_ref[...] = pv
        m_ref[...] = m_new

    n_full = (step * (tq // MOBA_BLOCK)) // KV_GROUP
    score(0, sa_ref, xa_ref)

    def pair(i, carry):
        score(2 * i + 1, sb_ref, xb_ref)
        absorb(sa_ref, xa_ref, 2 * i, False)
        score(2 * i + 2, sa_ref, xa_ref)
        absorb(sb_ref, xb_ref, 2 * i + 1, False)
        return carry

    lax.fori_loop(0, n_full // 2, pair, 0)

    @pl.when(n_full % 2 == 1)
    def _():
        score(n_full, sb_ref, xb_ref)
        absorb(sa_ref, xa_ref, n_full - 1, False)
        absorb(sb_ref, xb_ref, n_full, True)

    @pl.when(n_full % 2 == 0)
    def _():
        absorb(sa_ref, xa_ref, n_full, True)

    o_ref[...] = (acc_ref[:HEAD_DIM, :] / acc_ref[HEAD_DIM:HEAD_DIM + 1, :]).T.astype(o_ref.dtype)


def _attn_post_kernel(o_ref, z_ref, h_ref, wo_ref, out_ref, wob_ref):
    _cast_once(wo_ref, wob_ref)
    out = h_ref[...]
    for c in range(o_ref.shape[1] // MXU_COLS):
        cols = slice(c * MXU_COLS, (c + 1) * MXU_COLS)
        gated = o_ref[:, cols].astype(F32) * jax.nn.silu(z_ref[:, cols].astype(F32))
        out = out + jnp.dot(gated.astype(BF16), wob_ref[cols, :], preferred_element_type=F32)
    out_ref[...] = out


def _rope_tables(seq):
    half = ROT_DIM // 2
    inv_freq = ROPE_THETA ** (-(np.arange(half, dtype=np.float64) * 2.0) / ROT_DIM)
    ang = np.arange(seq, dtype=np.float64)[:, None] * inv_freq[None, :]
    cos, sin = np.cos(ang), np.sin(ang)
    pad = HEAD_DIM - ROT_DIM
    zeros = np.zeros((seq, half))
    cosf = np.concatenate([cos, cos, np.ones((seq, pad))], axis=1)
    sa = np.concatenate([-sin, zeros, np.zeros((seq, pad))], axis=1)
    sb = np.concatenate([zeros, sin, np.zeros((seq, pad))], axis=1)
    return tuple(jnp.asarray(t, F32) for t in (cosf, sa, sb))


def _moba_layer(h, g, w_in, q_gain, k_gain, w_out, rope):
    seq, d = h.shape
    n_head = d // HEAD_DIM
    n_blk = seq // MOBA_BLOCK
    assert n_blk <= LANES and n_blk % KV_GROUP == 0 and seq % MOBA_BLOCK == 0
    cosf, sa, sb = rope
    w_qkz = jnp.concatenate([w_in[:, :2 * d], w_in[:, 3 * d:]], axis=1).astype(BF16)
    w_vt = w_in[:, 2 * d:3 * d].astype(BF16).T

    tm = ROW_TILE
    bpt = tm // MOBA_BLOCK
    row = lambda i: (i, 0)
    const = lambda i: (0, 0)
    q, kaug, kmean, vt, z = pl.pallas_call(
        _attn_in_kernel,
        grid=(seq // tm,),
        in_specs=[pl.BlockSpec((tm, d), row),
                  pl.BlockSpec((1, d), const),
                  pl.BlockSpec((d, 3 * d), const),
                  pl.BlockSpec((d, d), const),
                  pl.BlockSpec((1, HEAD_DIM), const),
                  pl.BlockSpec((1, HEAD_DIM), const),
                  pl.BlockSpec((tm, HEAD_DIM), row),
                  pl.BlockSpec((tm, HEAD_DIM), row),
                  pl.BlockSpec((tm, HEAD_DIM), row)],
        out_specs=[pl.BlockSpec((n_head, tm, HEAD_DIM), lambda i: (0, i, 0)),
                   pl.BlockSpec((n_head, tm, 2 * HEAD_DIM), lambda i: (0, i, 0)),
                   pl.BlockSpec((bpt, n_head, HEAD_DIM), lambda i: (i, 0, 0)),
                   pl.BlockSpec((bpt, n_head * V_ROWS, MOBA_BLOCK), lambda i: (i, 0, 0)),
                   pl.BlockSpec((tm, d), row)],
        out_shape=[jax.ShapeDtypeStruct((n_head, seq, HEAD_DIM), F32),
                   jax.ShapeDtypeStruct((n_head, seq, 2 * HEAD_DIM), BF16),
                   jax.ShapeDtypeStruct((n_blk, n_head, HEAD_DIM), F32),
                   jax.ShapeDtypeStruct((n_blk, n_head * V_ROWS, MOBA_BLOCK), BF16),
                   jax.ShapeDtypeStruct((seq, d), BF16)],
        compiler_params=_cparams("parallel"),
        name="attn_in",
    )(h, g.reshape(1, d), w_qkz, w_vt, q_gain.reshape(1, HEAD_DIM), k_gain.reshape(1, HEAD_DIM),
      cosf, sa, sb)

    qa = pl.pallas_call(
        _moba_select_kernel,
        grid=(n_head, seq // SEL_TILE),
        in_specs=[pl.BlockSpec((None, SEL_TILE, HEAD_DIM), lambda hd, i: (hd, i, 0)),
                  pl.BlockSpec((None, n_blk, HEAD_DIM), lambda hd, i: (hd, 0, 0))],
        out_specs=pl.BlockSpec((None, 2 * HEAD_DIM, SEL_TILE), lambda hd, i: (hd, 0, i)),
        out_shape=jax.ShapeDtypeStruct((n_head, 2 * HEAD_DIM, seq), BF16),
        compiler_params=_cparams("parallel", "parallel"),
        name="moba_select",
    )(q, kmean.transpose(1, 0, 2))

    o = pl.pallas_call(
        _moba_kernel,
        grid=(n_head, seq // Q_TILE),
        in_specs=[pl.BlockSpec((None, 2 * HEAD_DIM, Q_TILE), lambda hd, i: (hd, 0, i)),
                  pl.BlockSpec((None, seq, 2 * HEAD_DIM), lambda hd, i: (hd, 0, 0)),
                  pl.BlockSpec((n_blk, V_ROWS, MOBA_BLOCK), lambda hd, i: (0, hd, 0))],
        out_specs=pl.BlockSpec((Q_TILE, HEAD_DIM), lambda hd, i: (i, hd)),
        out_shape=jax.ShapeDtypeStruct((seq, d), BF16),
        scratch_shapes=[pltpu.VMEM((KV_GROUP * MOBA_BLOCK, Q_TILE), F32),
                        pltpu.VMEM((KV_GROUP * MOBA_BLOCK, Q_TILE), F32),
                        pltpu.VMEM((1, Q_TILE), F32),
                        pltpu.VMEM((1, Q_TILE), F32),
                        pltpu.VMEM((1, Q_TILE), F32),
                        pltpu.VMEM((V_ROWS, Q_TILE), F32)],
        compiler_params=_cparams("parallel", "arbitrary"),
        name="moba_attn",
    )(qa, kaug, vt)

    tp = POST_TILE
    return pl.pallas_call(
        _attn_post_kernel,
        grid=(seq // tp,),
        in_specs=[pl.BlockSpec((tp, d), row),
                  pl.BlockSpec((tp, d), row),
                  pl.BlockSpec((tp, d), row),
                  pl.BlockSpec((d, d), const)],
        out_specs=pl.BlockSpec((tp, d), row),
        out_shape=jax.ShapeDtypeStruct((seq, d), F32),
        scratch_shapes=[pltpu.VMEM((d, d), BF16)],
        compiler_params=_cparams("arbitrary"),
        name="attn_post",
    )(o, z, h, w_out)


def kernel(x, norm_g, ssm_w_in, ssm_a_re, ssm_a_im, ssm_log_dt, ssm_b_re, ssm_b_im, ssm_c_re, ssm_c_im, ssm_d, ssm_w_glu, ssm_b_glu, ssm_w_out, attn_w_in, attn_q_gain, attn_k_gain, attn_w_out):
    bsz, seq, d = x.shape
    depth = norm_g.shape[0]
    rope = _rope_tables(seq)
    outs = []
    for b in range(bsz):
        h = x[b]
        for i in range(depth):
            j = i // 2
            if i % 2 == 0:
                h = _s5_layer(h, norm_g[i], ssm_w_in[j], ssm_a_re[j], ssm_a_im[j], ssm_log_dt[j],
                              ssm_b_re[j], ssm_b_im[j], ssm_c_re[j], ssm_c_im[j], ssm_d[j],
                              ssm_w_glu[j], ssm_b_glu[j], ssm_w_out[j])
            else:
                h = _moba_layer(h, norm_g[i], attn_w_in[j], attn_q_gain[j], attn_k_gain[j],
                                attn_w_out[j], rope)
        outs.append(h)
    return jnp.stack(outs)
```

```python
import jax
import jax.numpy as jnp
import numpy as np
from jax import lax
from jax.experimental import pallas as pl
from jax.experimental.pallas import tpu as pltpu

F32 = jnp.float32
BF16 = jnp.bfloat16

NORM_EPS = 1e-6
SSM_GROUP = 16
SSM_STATE = 64
HEAD_DIM = 128
ROT_DIM = HEAD_DIM // 4
ROPE_THETA = 500000.0
MOBA_BLOCK = 256
MOBA_TOPK = 3

LANES = 128
MXU_COLS = 256
SUBLANES = 8
SLAB_GROUPS = LANES // SSM_GROUP
SLAB_STATE = 2 * SLAB_GROUPS * SSM_STATE
SSM_CHUNK = 16
KV_GROUP = 4
NEG_BIG = -(2.0 ** 60)
LOG2_E = 1.4426950408889634
Q_TILE = 1024
V_ROWS = HEAD_DIM + 16
SEL_TILE = 2048
ROW_TILE = 512
POST_TILE = 1024
CHUNK_TILE = 512
SCAN_TILE = 256
VMEM_LIMIT = 56 * 1024 * 1024

_NT = (((1,), (1,)), ((), ()))


def _cparams(*sem):
    return pltpu.CompilerParams(dimension_semantics=sem, vmem_limit_bytes=VMEM_LIMIT)


def _rms(x, g):
    ms = jnp.mean(x * x, axis=-1, keepdims=True)
    return x * lax.rsqrt(ms + NORM_EPS) * g


def _log2(n):
    assert n & (n - 1) == 0
    return n.bit_length() - 1


def _cast_once(w_ref, wb_ref):
    @pl.when(pl.program_id(0) == 0)
    def _():
        wb_ref[...] = w_ref[...].astype(BF16)


def _s5_in_kernel(h_ref, g_ref, w_ref, u_ref, z_ref, us_ref, wb_ref):
    tm, d = h_ref.shape
    n_row = tm // SSM_CHUNK
    _cast_once(w_ref, wb_ref)
    hn = _rms(h_ref[...], g_ref[...]).astype(BF16)
    proj = jnp.dot(hn, wb_ref[...], preferred_element_type=F32)
    z_ref[...] = proj[:, d:].astype(BF16)
    for s in range(d // LANES):
        us_ref[s] = proj[:, s * LANES:(s + 1) * LANES]
        for t in range(SSM_CHUNK):
            u_ref[s, :, t * LANES:(t + 1) * LANES] = us_ref[s, pl.ds(t, n_row, stride=SSM_CHUNK), :]


def _expand_rows(re2, im2):
    r = lax.broadcasted_iota(jnp.int32, re2.shape, 0)
    lane = lax.broadcasted_iota(jnp.int32, re2.shape, 1)
    row_group = r >> _log2(SSM_GROUP)
    lane_half = lane >> _log2(SSM_STATE)
    cols_re, cols_im = [], []
    for j in range(SLAB_STATE // 2 // LANES):
        msk = row_group == 2 * j + lane_half
        cols_re.append(jnp.where(msk, re2, 0.0))
        cols_im.append(jnp.where(msk, im2, 0.0))
    return jnp.concatenate(cols_re + cols_im, axis=1)


def _ssm_state_kernel(u_ref, re2_ref, im2_ref, z_ref, wz_ref):
    @pl.when(pl.program_id(1) == 0)
    def _():
        for k in range(re2_ref.shape[0]):
            wz_ref[k * LANES:(k + 1) * LANES, :] = _expand_rows(re2_ref[k], im2_ref[k]).astype(BF16)

    z_ref[...] = jnp.dot(u_ref[...].astype(BF16), wz_ref[...], preferred_element_type=F32)


def _ssm_scan_kernel(z_ref, ap_ref, o_ref, last_ref):
    half = SLAB_STATE // 2
    n_slab = z_ref.shape[1] // SLAB_STATE

    @pl.when(pl.program_id(0) == 0)
    def _():
        last_ref[...] = jnp.zeros_like(last_ref)

    rows = lax.broadcasted_iota(jnp.int32, (SUBLANES, half), 0)

    def shift_down(x, k, fill):
        return jnp.where(rows >= k, pltpu.roll(x, k, 0), fill)

    def body(step, _):
        r0 = pl.multiple_of(step * SUBLANES, SUBLANES)
        for s in range(n_slab):
            re_sl = slice(s * SLAB_STATE, s * SLAB_STATE + half)
            im_sl = slice(s * SLAB_STATE + half, (s + 1) * SLAB_STATE)
            xr = z_ref[pl.ds(r0, SUBLANES), re_sl]
            xi = z_ref[pl.ds(r0, SUBLANES), im_sl]
            for k in (1, 2, 4):
                ar = ap_ref[k - 1:k, re_sl]
                ai = ap_ref[k - 1:k, im_sl]
                sr, si = shift_down(xr, k, 0.0), shift_down(xi, k, 0.0)
                xr, xi = xr + ar * sr - ai * si, xi + ar * si + ai * sr
            cr = jnp.broadcast_to(last_ref[SUBLANES - 1:SUBLANES, re_sl], (SUBLANES, half))
            ci = jnp.broadcast_to(last_ref[SUBLANES - 1:SUBLANES, im_sl], (SUBLANES, half))
            apr, api = ap_ref[:, re_sl], ap_ref[:, im_sl]
            xr, xi = xr + apr * cr - api * ci, xi + apr * ci + api * cr
            o_ref[pl.ds(r0, SUBLANES), re_sl] = shift_down(xr, 1, cr)
            o_ref[pl.ds(r0, SUBLANES), im_sl] = shift_down(xi, 1, ci)
            last_ref[:, re_sl] = xr
            last_ref[:, im_sl] = xi
        return 0

    lax.fori_loop(0, z_ref.shape[0] // SUBLANES, body, 0)


def _ssm_out_kernel(u_ref, sp_ref, cre_ref, cim_ref, fre_ref, fim_ref, d_ref, y_ref,
                    m_ref, ct_ref, clo_ref):
    t_len = SSM_CHUNK
    tr = u_ref.shape[0]
    tw = t_len * LANES

    @pl.when(pl.program_id(1) == 0)
    def _():
        for t in range(t_len + 1):
            x = _expand_rows(cre_ref[t], cim_ref[t])
            hi = x.astype(BF16)
            ct_ref[t * LANES:(t + 1) * LANES, :] = hi
            if t < t_len:
                clo_ref[t * LANES:(t + 1) * LANES, :] = (x - hi.astype(F32)).astype(BF16)
        fb = _expand_rows(fre_ref[...], fim_ref[...])
        fb_hi = fb.astype(BF16)
        fb_lo = (fb - fb_hi.astype(F32)).astype(BF16)
        c_hi, c_lo = ct_ref[:tw, :], clo_ref[...]
        krow = (lax.dot_general(fb_hi, c_hi, _NT, preferred_element_type=F32)
                + lax.dot_general(fb_lo, c_hi, _NT, preferred_element_type=F32)
                + lax.dot_general(fb_hi, c_lo, _NT, preferred_element_type=F32)).astype(BF16)
        for k in range(t_len):
            parts = [krow[:, :(t_len - k) * LANES]]
            if k:
                parts = [jnp.zeros((LANES, k * LANES), BF16)] + parts
            m_ref[k * LANES:(k + 1) * LANES, :] = jnp.concatenate(parts, axis=1)

    u = u_ref[...]
    ub = u.astype(BF16)
    sp = sp_ref[...].astype(BF16)
    for j in range(t_len // 2):
        lo, hi = 2 * j * LANES, (2 * j + 2) * LANES
        y = jnp.dot(ub[:, :hi], m_ref[:hi, lo:hi], preferred_element_type=F32)
        y = y + lax.dot_general(sp, ct_ref[LANES + lo:LANES + hi, :], _NT, preferred_element_type=F32)
        y = jax.nn.gelu(y + d_ref[:, lo:hi] * u[:, lo:hi], approximate=True)
        for t in (2 * j, 2 * j + 1):
            y_ref[pl.ds(t, tr, stride=t_len), :] = y[:, t * LANES - lo:(t + 1) * LANES - lo]


def _s5_post_kernel(y_ref, z_ref, h_ref, wg_ref, bg_ref, wo_ref, o_ref, wgb_ref, wob_ref):
    _cast_once(wg_ref, wgb_ref)
    _cast_once(wo_ref, wob_ref)
    y = jnp.concatenate([y_ref[s] for s in range(y_ref.shape[0])], axis=-1)
    lin = jnp.dot(y.astype(BF16), wgb_ref[...], preferred_element_type=F32) + bg_ref[...]
    gated = y * jax.nn.sigmoid(lin) * jax.nn.silu(z_ref[...].astype(F32))
    o_ref[...] = h_ref[...] + jnp.dot(gated.astype(BF16), wob_ref[...], preferred_element_type=F32)


def _s5_derived(a_re, a_im, log_dt, b_re, b_im, c_re, c_im, d_skip):
    t_len = SSM_CHUNK
    g_cnt, p_cnt = a_re.shape
    n_slab = g_cnt // SLAB_GROUPS
    dt = jnp.exp(log_dt)[:, None]

    def dup(x):
        return jnp.concatenate([x, x], axis=-1)

    def powers(exps, lam_re, lam_im):
        j = exps.astype(F32)[:, None, None]
        mag_j = jnp.exp(j * lam_re)
        return mag_j * jnp.cos(j * lam_im), mag_j * jnp.sin(j * lam_im)

    lr, li = dup(a_re), dup(a_im)
    lam_re, lam_im = lr * dt, li * dt
    mag = jnp.exp(lam_re)
    ab_re, ab_im = mag * jnp.cos(lam_im), mag * jnp.sin(lam_im)
    den = lr * lr + li * li
    nr, ni = ab_re - 1.0, ab_im
    f_re = (nr * lr + ni * li) / den
    f_im = (ni * lr - nr * li) / den
    pr, pi = powers(t_len - 1 - jnp.arange(t_len), lam_re, lam_im)
    w_re = (pr * f_re - pi * f_im)[:, :, None, :]
    w_im = (pr * f_im + pi * f_re)[:, :, None, :]
    bt_re, bt_im = dup(b_re.transpose(0, 2, 1)), dup(b_im.transpose(0, 2, 1))
    wz_re = w_re * bt_re - w_im * bt_im
    wz_im = w_re * bt_im + w_im * bt_re
    qr, qi = powers(jnp.arange(t_len + 1), lam_re, lam_im)
    qr, qi = qr[:, :, None, :], qi[:, :, None, :]
    ct_re, ct_im = dup(c_re), dup(c_im)
    cv_re = ct_re * qr - ct_im * qi
    cv_im = -(ct_re * qi + ct_im * qr)

    def rows(x):
        return x.reshape(x.shape[0], g_cnt * SSM_GROUP, 2 * p_cnt)

    ap_re, ap_im = powers(t_len * jnp.arange(1, SUBLANES + 1), a_re * dt, a_im * dt)
    a_pow = jnp.stack([ap_re.reshape(SUBLANES, n_slab, -1), ap_im.reshape(SUBLANES, n_slab, -1)], axis=2)
    a_pow = a_pow.reshape(SUBLANES, n_slab * SLAB_STATE)
    d_til = jnp.tile(d_skip.reshape(n_slab, 1, LANES), (1, 1, t_len))
    return rows(wz_re), rows(wz_im), rows(cv_re), rows(cv_im), a_pow, d_til


def _s5_layer(h, g, layer, w_in, a_re, a_im, log_dt, b_re, b_im, c_re, c_im, d_skip, w_glu, b_glu, w_out):
    seq, d = h.shape
    n_slab = d // LANES
    t_len = SSM_CHUNK
    n_chunk = seq // t_len
    tw = t_len * LANES
    sw = SLAB_STATE
    wz_re, wz_im, cv_re, cv_im, a_pow, d_til = _s5_derived(
        a_re, a_im, log_dt, b_re, b_im, c_re, c_im, d_skip)

    tm = ROW_TILE
    uc, z = pl.pallas_call(
        _s5_in_kernel,
        grid=(seq // tm,),
        in_specs=[pl.BlockSpec((tm, d), lambda i: (i, 0)),
                  pl.BlockSpec((1, d), lambda i: (0, 0)),
                  pl.BlockSpec((None, d, 2 * d), lambda i: (layer, 0, 0))],
        out_specs=[pl.BlockSpec((n_slab, tm // t_len, tw), lambda i: (0, i, 0)),
                   pl.BlockSpec((tm, d), lambda i: (i, 0))],
        out_shape=[jax.ShapeDtypeStruct((n_slab, n_chunk, tw), F32),
                   jax.ShapeDtypeStruct((seq, d), BF16)],
        scratch_shapes=[pltpu.VMEM((n_slab, tm, LANES), F32), pltpu.VMEM((d, 2 * d), BF16)],
        compiler_params=_cparams("arbitrary"),
        name="s5_in",
    )(h, g.reshape(1, d), w_in)

    tr = CHUNK_TILE
    steps = lambda s, i: (0, s, 0)
    zst = pl.pallas_call(
        _ssm_state_kernel,
        grid=(n_slab, n_chunk // tr),
        in_specs=[pl.BlockSpec((None, tr, tw), lambda s, i: (s, i, 0)),
                  pl.BlockSpec((t_len, LANES, LANES), steps),
                  pl.BlockSpec((t_len, LANES, LANES), steps)],
        out_specs=pl.BlockSpec((tr, sw), lambda s, i: (i, s)),
        out_shape=jax.ShapeDtypeStruct((n_chunk, n_slab * sw), F32),
        scratch_shapes=[pltpu.VMEM((tw, sw), BF16)],
        compiler_params=_cparams("parallel", "arbitrary"),
        name="ssm_state",
    )(uc, wz_re, wz_im)

    tc = SCAN_TILE
    sprev = pl.pallas_call(
        _ssm_scan_kernel,
        grid=(n_chunk // tc,),
        in_specs=[pl.BlockSpec((tc, n_slab * sw), lambda i: (i, 0)),
                  pl.BlockSpec((SUBLANES, n_slab * sw), lambda i: (0, 0))],
        out_specs=pl.BlockSpec((tc, n_slab * sw), lambda i: (i, 0)),
        out_shape=jax.ShapeDtypeStruct((n_chunk, n_slab * sw), F32),
        scratch_shapes=[pltpu.VMEM((SUBLANES, n_slab * sw), F32)],
        compiler_params=_cparams("arbitrary"),
        name="ssm_scan",
    )(zst, a_pow)

    last_step = lambda s, i: (t_len - 1, s, 0)
    yg = pl.pallas_call(
        _ssm_out_kernel,
        grid=(n_slab, n_chunk // tr),
        in_specs=[pl.BlockSpec((None, tr, tw), lambda s, i: (s, i, 0)),
                  pl.BlockSpec((tr, sw), lambda s, i: (i, s)),
                  pl.BlockSpec((t_len + 1, LANES, LANES), steps),
                  pl.BlockSpec((t_len + 1, LANES, LANES), steps),
                  pl.BlockSpec((None, LANES, LANES), last_step),
                  pl.BlockSpec((None, LANES, LANES), last_step),
                  pl.BlockSpec((None, 1, tw), lambda s, i: (s, 0, 0))],
        out_specs=pl.BlockSpec((None, tr * t_len, LANES), lambda s, i: (s, i, 0)),
        out_shape=jax.ShapeDtypeStruct((n_slab, seq, LANES), F32),
        scratch_shapes=[pltpu.VMEM((tw, tw), BF16),
                        pltpu.VMEM((tw + LANES, sw), BF16),
                        pltpu.VMEM((tw, sw), BF16)],
        compiler_params=_cparams("parallel", "arbitrary"),
        name="ssm_out",
    )(uc, sprev, cv_re, cv_im, wz_re, wz_im, d_til)

    tp = POST_TILE
    return pl.pallas_call(
        _s5_post_kernel,
        grid=(seq // tp,),
        in_specs=[pl.BlockSpec((n_slab, tp, LANES), lambda i: (0, i, 0)),
                  pl.BlockSpec((tp, d), lambda i: (i, 0)),
                  pl.BlockSpec((tp, d), lambda i: (i, 0)),
                  pl.BlockSpec((None, d, d), lambda i: (layer, 0, 0)),
                  pl.BlockSpec((1, d), lambda i: (0, 0)),
                  pl.BlockSpec((None, d, d), lambda i: (layer, 0, 0))],
        out_specs=pl.BlockSpec((tp, d), lambda i: (i, 0)),
        out_shape=jax.ShapeDtypeStruct((seq, d), F32),
        scratch_shapes=[pltpu.VMEM((d, d), BF16), pltpu.VMEM((d, d), BF16)],
        compiler_params=_cparams("arbitrary"),
        name="s5_post",
    )(yg, z, h, w_glu, b_glu.reshape(1, d), w_out)


def _attn_in_kernel(h_ref, g_ref, wqkz_ref, wvt_ref, qg_ref, kg_ref, cos_ref, sa_ref, sb_ref,
                    q_ref, ka_ref, km_ref, vt_ref, z_ref):
    tm, d = h_ref.shape
    n_head = d // HEAD_DIM
    n_blk = tm // MOBA_BLOCK
    i = pl.program_id(0)
    hn = _rms(h_ref[...], g_ref[...]).astype(BF16)
    cosf, sa, sb = cos_ref[...], sa_ref[...], sb_ref[...]

    def proj(col, width):
        return jnp.dot(hn, wqkz_ref[:, col:col + width], preferred_element_type=F32)

    def norm_rope(t, gain):
        t = _rms(t, gain)
        up = pltpu.roll(t, HEAD_DIM - ROT_DIM // 2, 1)
        dn = pltpu.roll(t, ROT_DIM // 2, 1)
        return t * cosf + up * sa + dn * sb

    scale = HEAD_DIM ** -0.5 * LOG2_E
    pairw = 2 * HEAD_DIM
    for hp in range(n_head // 2):
        qp = proj(hp * pairw, pairw)
        kp = proj(d + hp * pairw, pairw)
        for sub in range(2):
            hd = 2 * hp + sub
            qh = norm_rope(qp[:, sub * HEAD_DIM:(sub + 1) * HEAD_DIM], qg_ref[...])
            q_ref[hd] = qh * scale
            kh = norm_rope(kp[:, sub * HEAD_DIM:(sub + 1) * HEAD_DIM], kg_ref[...])
            ka_ref[hd, :, :HEAD_DIM] = kh.astype(BF16)
            for b in range(n_blk):
                km_ref[b, hd:hd + 1, :] = jnp.mean(
                    kh[b * MOBA_BLOCK:(b + 1) * MOBA_BLOCK], axis=0, keepdims=True)
    vt = lax.dot_general(wvt_ref[...], hn, _NT, preferred_element_type=F32)
    lane = lax.broadcasted_iota(jnp.int32, (MOBA_BLOCK, LANES), 1)
    pad_row = lax.broadcasted_iota(jnp.int32, (V_ROWS - HEAD_DIM, MOBA_BLOCK), 0)
    ones_row = jnp.where(pad_row == 0, 1.0, 0.0).astype(BF16)
    for b in range(n_blk):
        onehot = jnp.where(lane == i * n_blk + b, NEG_BIG, 0.0).astype(BF16)
        for hd in range(n_head):
            ka_ref[hd, b * MOBA_BLOCK:(b + 1) * MOBA_BLOCK, HEAD_DIM:] = onehot
        for hd in range(n_head):
            vt_ref[b, hd * V_ROWS:hd * V_ROWS + HEAD_DIM, :] = vt[
                hd * HEAD_DIM:(hd + 1) * HEAD_DIM, b * MOBA_BLOCK:(b + 1) * MOBA_BLOCK].astype(BF16)
            vt_ref[b, hd * V_ROWS + HEAD_DIM:(hd + 1) * V_ROWS, :] = ones_row
    z_ref[...] = proj(2 * d, d).astype(BF16)


def _moba_select_kernel(q_ref, km_ref, qa_ref):
    ts, hd = q_ref.shape
    n_blk = km_ref.shape[0]
    q = q_ref[...]
    gates = lax.dot_general(km_ref[...], q, _NT, precision=lax.Precision.HIGHEST,
                            preferred_element_type=F32)
    nidx = lax.broadcasted_iota(jnp.int32, gates.shape, 0)
    qpos = pl.program_id(1) * ts + lax.broadcasted_iota(jnp.int32, gates.shape, 1)
    own = qpos >> _log2(MOBA_BLOCK)
    valid = nidx < own
    g = jnp.where(valid, gates, -jnp.inf)
    picked = jnp.zeros(gates.shape, jnp.bool_)
    for _ in range(MOBA_TOPK):
        mx = jnp.max(g, axis=0, keepdims=True)
        first = jnp.min(jnp.where(g == mx, nidx, n_blk), axis=0, keepdims=True)
        hit = nidx == first
        picked = jnp.logical_or(picked, hit)
        g = jnp.where(hit, -jnp.inf, g)
    visible = jnp.logical_or(jnp.logical_and(valid, picked), nidx == own)
    qa_ref[:hd, :] = q.T.astype(BF16)
    qa_ref[hd:hd + n_blk, :] = jnp.where(visible, 0.0, 1.0).astype(BF16)
    qa_ref[hd + n_blk:, :] = jnp.zeros((qa_ref.shape[0] - hd - n_blk, ts), BF16)


def _moba_kernel(qa_ref, k_ref, vt_ref, o_ref, sa_ref, sb_ref, xa_ref, xb_ref, m_ref, acc_ref):
    step = pl.program_id(1)
    tq = qa_ref.shape[1]
    gk = KV_GROUP * MOBA_BLOCK
    assert KV_GROUP % (tq // MOBA_BLOCK) == 0

    m_ref[...] = jnp.full(m_ref.shape, -jnp.inf, F32)
    acc_ref[...] = jnp.zeros_like(acc_ref)

    def score(gi, dst_ref, max_ref):
        kt = k_ref[pl.ds(pl.multiple_of(gi * gk, gk), gk), :]
        s = jnp.dot(kt, qa_ref[...], preferred_element_type=F32)
        dst_ref[...] = s
        max_ref[...] = jnp.max(s, axis=0, keepdims=True)

    def rows(b):
        return slice(b * MOBA_BLOCK, (b + 1) * MOBA_BLOCK)

    def absorb(src_ref, max_ref, gi, causal):
        if causal and tq == gk:
            tri = (lax.broadcasted_iota(jnp.int32, (MOBA_BLOCK, MOBA_BLOCK), 0)
                   <= lax.broadcasted_iota(jnp.int32, (MOBA_BLOCK, MOBA_BLOCK), 1))
            for b in range(KV_GROUP):
                src_ref[rows(b), rows(b)] = jnp.where(tri, src_ref[rows(b), rows(b)], -jnp.inf)
        elif causal:
            kpos = gi * gk + lax.broadcasted_iota(jnp.int32, src_ref.shape, 0)
            qpos = step * tq + lax.broadcasted_iota(jnp.int32, src_ref.shape, 1)
            src_ref[...] = jnp.where(kpos <= qpos, src_ref[...], -jnp.inf)
        m_old = m_ref[...]
        if causal:
            m_new = m_old
            for b in range(KV_GROUP):
                m_new = jnp.maximum(m_new, jnp.max(src_ref[rows(b), :], axis=0, keepdims=True))
        else:
            m_new = jnp.maximum(m_old, max_ref[...])
        alpha = jnp.exp2(m_old - m_new)
        pv = alpha * acc_ref[...]
        for b in range(KV_GROUP):
            p = jnp.exp2(src_ref[rows(b), :] - m_new)
            pv = pv + jnp.dot(vt_ref[gi * KV_GROUP + b], p.astype(BF16), preferred_element_type=F32)
        acc_ref[...] = pv
        m_ref[...] = m_new

    n_full = (step * (tq // MOBA_BLOCK)) // KV_GROUP
    score(0, sa_ref, xa_ref)

    def pair(i, carry):
        score(2 * i + 1, sb_ref, xb_ref)
        absorb(sa_ref, xa_ref, 2 * i, False)
        score(2 * i + 2, sa_ref, xa_ref)
        absorb(sb_ref, xb_ref, 2 * i + 1, False)
        return carry

    lax.fori_loop(0, n_full // 2, pair, 0)

    @pl.when(n_full % 2 == 1)
    def _():
        score(n_full, sb_ref, xb_ref)
        absorb(sa_ref, xa_ref, n_full - 1, False)
        absorb(sb_ref, xb_ref, n_full, True)

    @pl.when(n_full % 2 == 0)
    def _():
        absorb(sa_ref, xa_ref, n_full, True)

    o_ref[...] = (acc_ref[:HEAD_DIM, :] / acc_ref[HEAD_DIM:HEAD_DIM + 1, :]).T.astype(o_ref.dtype)


def _attn_post_kernel(o_ref, z_ref, h_ref, wo_ref, out_ref, wob_ref):
    _cast_once(wo_ref, wob_ref)
    out = h_ref[...]
    for c in range(o_ref.shape[1] // MXU_COLS):
        cols = slice(c * MXU_COLS, (c + 1) * MXU_COLS)
        gated = o_ref[:, cols].astype(F32) * jax.nn.silu(z_ref[:, cols].astype(F32))
        out = out + jnp.dot(gated.astype(BF16), wob_ref[cols, :], preferred_element_type=F32)
    out_ref[...] = out


def _rope_tables(seq):
    half = ROT_DIM // 2
    inv_freq = ROPE_THETA ** (-(np.arange(half, dtype=np.float64) * 2.0) / ROT_DIM)
    ang = np.arange(seq, dtype=np.float64)[:, None] * inv_freq[None, :]
    cos, sin = np.cos(ang), np.sin(ang)
    pad = HEAD_DIM - ROT_DIM
    zeros = np.zeros((seq, half))
    cosf = np.concatenate([cos, cos, np.ones((seq, pad))], axis=1)
    sa = np.concatenate([-sin, zeros, np.zeros((seq, pad))], axis=1)
    sb = np.concatenate([zeros, sin, np.zeros((seq, pad))], axis=1)
    return tuple(jnp.asarray(t, F32) for t in (cosf, sa, sb))


def _moba_layer(h, g, layer, w_in, q_gain, k_gain, w_out, rope):
    seq, d = h.shape
    n_head = d // HEAD_DIM
    n_blk = seq // MOBA_BLOCK
    assert n_blk <= LANES and n_blk % KV_GROUP == 0 and seq % MOBA_BLOCK == 0
    cosf, sa, sb = rope
    w_qkz = jnp.concatenate([w_in[:, :2 * d], w_in[:, 3 * d:]], axis=1).astype(BF16)
    w_vt = w_in[:, 2 * d:3 * d].astype(BF16).T

    tm = ROW_TILE
    bpt = tm // MOBA_BLOCK
    row = lambda i: (i, 0)
    const = lambda i: (0, 0)
    q, kaug, kmean, vt, z = pl.pallas_call(
        _attn_in_kernel,
        grid=(seq // tm,),
        in_specs=[pl.BlockSpec((tm, d), row),
                  pl.BlockSpec((1, d), const),
                  pl.BlockSpec((d, 3 * d), const),
                  pl.BlockSpec((d, d), const),
                  pl.BlockSpec((1, HEAD_DIM), const),
                  pl.BlockSpec((1, HEAD_DIM), const),
                  pl.BlockSpec((tm, HEAD_DIM), row),
                  pl.BlockSpec((tm, HEAD_DIM), row),
                  pl.BlockSpec((tm, HEAD_DIM), row)],
        out_specs=[pl.BlockSpec((n_head, tm, HEAD_DIM), lambda i: (0, i, 0)),
                   pl.BlockSpec((n_head, tm, 2 * HEAD_DIM), lambda i: (0, i, 0)),
                   pl.BlockSpec((bpt, n_head, HEAD_DIM), lambda i: (i, 0, 0)),
                   pl.BlockSpec((bpt, n_head * V_ROWS, MOBA_BLOCK), lambda i: (i, 0, 0)),
                   pl.BlockSpec((tm, d), row)],
        out_shape=[jax.ShapeDtypeStruct((n_head, seq, HEAD_DIM), F32),
                   jax.ShapeDtypeStruct((n_head, seq, 2 * HEAD_DIM), BF16),
                   jax.ShapeDtypeStruct((n_blk, n_head, HEAD_DIM), F32),
                   jax.ShapeDtypeStruct((n_blk, n_head * V_ROWS, MOBA_BLOCK), BF16),
                   jax.ShapeDtypeStruct((seq, d), BF16)],
        compiler_params=_cparams("parallel"),
        name="attn_in",
    )(h, g.reshape(1, d), w_qkz, w_vt, q_gain.reshape(1, HEAD_DIM), k_gain.reshape(1, HEAD_DIM),
      cosf, sa, sb)

    qa = pl.pallas_call(
        _moba_select_kernel,
        grid=(n_head, seq // SEL_TILE),
        in_specs=[pl.BlockSpec((None, SEL_TILE, HEAD_DIM), lambda hd, i: (hd, i, 0)),
                  pl.BlockSpec((None, n_blk, HEAD_DIM), lambda hd, i: (hd, 0, 0))],
        out_specs=pl.BlockSpec((None, 2 * HEAD_DIM, SEL_TILE), lambda hd, i: (hd, 0, i)),
        out_shape=jax.ShapeDtypeStruct((n_head, 2 * HEAD_DIM, seq), BF16),
        compiler_params=_cparams("parallel", "parallel"),
        name="moba_select",
    )(q, kmean.transpose(1, 0, 2))

    o = pl.pallas_call(
        _moba_kernel,
        grid=(n_head, seq // Q_TILE),
        in_specs=[pl.BlockSpec((None, 2 * HEAD_DIM, Q_TILE), lambda hd, i: (hd, 0, i)),
                  pl.BlockSpec((None, seq, 2 * HEAD_DIM), lambda hd, i: (hd, 0, 0)),
                  pl.BlockSpec((n_blk, V_ROWS, MOBA_BLOCK), lambda hd, i: (0, hd, 0))],
        out_specs=pl.BlockSpec((Q_TILE, HEAD_DIM), lambda hd, i: (i, hd)),
        out_shape=jax.ShapeDtypeStruct((seq, d), BF16),
        scratch_shapes=[pltpu.VMEM((KV_GROUP * MOBA_BLOCK, Q_TILE), F32),
                        pltpu.VMEM((KV_GROUP * MOBA_BLOCK, Q_TILE), F32),
                        pltpu.VMEM((1, Q_TILE), F32),
                        pltpu.VMEM((1, Q_TILE), F32),
                        pltpu.VMEM((1, Q_TILE), F32),
                        pltpu.VMEM((V_ROWS, Q_TILE), F32)],
        compiler_params=_cparams("parallel", "arbitrary"),
        name="moba_attn",
    )(qa, kaug, vt)

    tp = POST_TILE
    return pl.pallas_call(
        _attn_post_kernel,
        grid=(seq // tp,),
        in_specs=[pl.BlockSpec((tp, d), row),
                  pl.BlockSpec((tp, d), row),
                  pl.BlockSpec((tp, d), row),
                  pl.BlockSpec((None, d, d), lambda i: (layer, 0, 0))],
        out_specs=pl.BlockSpec((tp, d), row),
        out_shape=jax.ShapeDtypeStruct((seq, d), F32),
        scratch_shapes=[pltpu.VMEM((d, d), BF16)],
        compiler_params=_cparams("arbitrary"),
        name="attn_post",
    )(o, z, h, w_out)


def kernel(x, norm_g, ssm_w_in, ssm_a_re, ssm_a_im, ssm_log_dt, ssm_b_re, ssm_b_im, ssm_c_re, ssm_c_im, ssm_d, ssm_w_glu, ssm_b_glu, ssm_w_out, attn_w_in, attn_q_gain, attn_k_gain, attn_w_out):
    bsz, seq, d = x.shape
    depth = norm_g.shape[0]
    rope = _rope_tables(seq)
    outs = []
    for b in range(bsz):
        h = x[b]
        for i in range(depth):
            j = i // 2
            if i % 2 == 0:
                h = _s5_layer(h, norm_g[i], j, ssm_w_in, ssm_a_re[j], ssm_a_im[j], ssm_log_dt[j],
                              ssm_b_re[j], ssm_b_im[j], ssm_c_re[j], ssm_c_im[j], ssm_d[j],
                              ssm_w_glu, ssm_b_glu[j], ssm_w_out)
            else:
                h = _moba_layer(h, norm_g[i], j, attn_w_in[j], attn_q_gain[j], attn_k_gain[j],
                                attn_w_out, rope)
        outs.append(h)
    return jnp.stack(outs)
```
